```python
import math, functools
import jax, jax.numpy as jnp
from jax import lax
import numpy as np

D_MODEL = 1024
BATCH = 2
SEQ = 16384
DEPTH = 1
DEC_BATCH = 128
DEC_SEQ = 1
PAST_LEN = 8192
PAGE_SIZE = 128

D_MIX = D_MODEL
D_ATTN = D_MIX // 2
D_SSM = D_MIX - D_ATTN
HEAD_DIM = 64
N_HEADS = D_ATTN // HEAD_DIM
SSM_GROUP = 16
N_SSM_GROUPS = D_SSM // SSM_GROUP
SSM_STATE = 64
D_FF = ((8 * D_MODEL // 3 + 255) // 256) * 256
D_IN = 3 * D_ATTN + N_HEADS + D_SSM
QUERY_BLOCK = 128
EPS = 1e-6
NEG = -1e30
DT_MIN = 1e-3
DT_MAX = 1e-1
F32 = jnp.float32

kernel_name = "hymba_fox_s5_sandwich_adaln_step"


def rms_norm(x, g):
    xf = x.astype(F32)
    return xf * lax.rsqrt(jnp.mean(xf * xf, axis=-1, keepdims=True) + EPS) * g.astype(F32)


def modulate(x, g, shift, scale):
    return rms_norm(x, g) * (1.0 + scale[:, None, :]) + shift[:, None, :]


def fox_prompt(q, k, v, logf):
    B, S = q.shape[:2]
    nqb = S // QUERY_BLOCK
    F = jnp.cumsum(logf.astype(F32), axis=1)
    qf = q.astype(F32) * (HEAD_DIM ** -0.5)
    kf = k.astype(F32)
    vf = v.astype(F32)
    Fk = F.transpose(0, 2, 1)
    qb = qf.reshape(B, nqb, QUERY_BLOCK, N_HEADS, HEAD_DIM).transpose(1, 0, 2, 3, 4)
    Fqb = F.reshape(B, nqb, QUERY_BLOCK, N_HEADS).transpose(1, 0, 3, 2)
    key_pos = jnp.arange(S)

    def block(args):
        qi, qblk, fq = args
        s = jnp.einsum("bqhd,bkhd->bhqk", qblk, kf) + fq[..., None] - Fk[:, :, None, :]
        q_pos = qi * QUERY_BLOCK + jnp.arange(QUERY_BLOCK)
        s = jnp.where(key_pos[None, :] <= q_pos[:, None], s, NEG)
        p = jax.nn.softmax(s, axis=-1)
        return jnp.einsum("bhqk,bkhd->bqhd", p, vf)

    out = lax.map(block, (jnp.arange(nqb), qb, Fqb))
    return out.transpose(1, 0, 2, 3, 4).reshape(B, S, D_ATTN)


def _online_update(carry, s, v):
    m, l, acc = carry
    m_new = jnp.maximum(m, s.max(-1))
    p = jnp.exp(s - m_new[..., None])
    corr = jnp.exp(m - m_new)
    return (m_new, l * corr + p.sum(-1),
            acc * corr[..., None] + jnp.einsum("bhts,bshd->bhtd", p, v))


def fox_sample(q, k, v, logf, cache_k, cache_v, cache_logf, page_table, layer):
    DB, T = q.shape[:2]
    n_pages = page_table.shape[1]
    past = n_pages * PAGE_SIZE
    past_logf = cache_logf[layer, page_table].reshape(DB, past, N_HEADS).astype(F32)
    F = jnp.cumsum(jnp.concatenate([past_logf, logf.astype(F32)], axis=1), axis=1)
    Fq = F[:, past:].transpose(0, 2, 1)
    F_past = F[:, :past].reshape(DB, n_pages, PAGE_SIZE, N_HEADS).transpose(1, 0, 3, 2)
    qf = q.astype(F32) * (HEAD_DIM ** -0.5)

    def page_step(carry, xs):
        phys, fk = xs
        kp = cache_k[layer, phys].astype(F32)
        vp = cache_v[layer, phys].astype(F32)
        s = jnp.einsum("bthd,bshd->bhts", qf, kp) + Fq[..., None] - fk[:, :, None, :]
        return _online_update(carry, s, vp), None

    init = (jnp.full((DB, N_HEADS, T), NEG, F32),
            jnp.zeros((DB, N_HEADS, T), F32),
            jnp.zeros((DB, N_HEADS, T, HEAD_DIM), F32))
    carry, _ = lax.scan(page_step, init, (page_table.T, F_past))
    s = jnp.einsum("bthd,bshd->bhts", qf, k.astype(F32)) + Fq[..., None] - Fq[:, :, None, :]
    s = jnp.where(jnp.tril(jnp.ones((T, T), dtype=bool)), s, NEG)
    m, l, acc = _online_update(carry, s, v.astype(F32))
    out = acc / l[..., None]
    return out.transpose(0, 2, 1, 3).reshape(DB, T, D_ATTN)


def s5_mixer(u, h0_re, h0_im, a_re, a_im, log_dt, b_re, b_im, c_re, c_im, d_skip, w_glu, b_glu):
    B, S = u.shape[:2]
    lam = lax.complex(a_re.astype(F32), a_im.astype(F32))
    dt = jnp.exp(log_dt.astype(F32))[:, None]
    lam_bar = jnp.exp(lam * dt)
    b = lax.complex(b_re.astype(F32), b_im.astype(F32))
    b_bar = ((lam_bar - 1.0) / lam)[..., None] * b
    c = lax.complex(c_re.astype(F32), c_im.astype(F32))
    uf = u.astype(F32).reshape(B, S, N_SSM_GROUPS, SSM_GROUP)
    bu = jnp.einsum("bsgh,gph->bsgp", uf.astype(jnp.complex64), b_bar)
    h0 = lax.complex(h0_re.astype(F32), h0_im.astype(F32))
    bu = bu.at[:, 0].add(lam_bar * h0)
    a = jnp.broadcast_to(lam_bar, bu.shape)

    def combine(e1, e2):
        a1, b1 = e1
        a2, b2 = e2
        return (a1 * a2, a2 * b1 + b2)

    _, xs = lax.associative_scan(combine, (a, bu), axis=1)
    y = jnp.einsum("bsgp,ghp->bsgh", xs, c).real + d_skip.astype(F32) * uf
    y = jax.nn.gelu(y.reshape(B, S, D_SSM))
    y = y * jax.nn.sigmoid(y @ w_glu.astype(F32) + b_glu.astype(F32))
    h_last = xs[:, -1]
    return y, h_last.real, h_last.imag


def layer_forward(x, cond, attend, h0_re, h0_im, p):
    B, S = x.shape[:2]
    mod = jax.nn.silu(cond.astype(F32)) @ p["w_ada"].astype(F32) + p["b_ada"].astype(F32)
    sh_m, sc_m, gt_m, sh_f, sc_f, gt_f = jnp.split(mod, 6, axis=-1)
    h = modulate(x, p["g_pre_mix"], sh_m, sc_m)
    z = h @ p["w_in"].astype(F32)
    q = z[..., :D_ATTN].reshape(B, S, N_HEADS, HEAD_DIM)
    k = z[..., D_ATTN:2 * D_ATTN].reshape(B, S, N_HEADS, HEAD_DIM)
    v = z[..., 2 * D_ATTN:3 * D_ATTN].reshape(B, S, N_HEADS, HEAD_DIM)
    f_logit = z[..., 3 * D_ATTN:3 * D_ATTN + N_HEADS]
    u = z[..., 3 * D_ATTN + N_HEADS:]
    logf = jax.nn.log_sigmoid(f_logit + p["b_f"].astype(F32))
    attn = attend(q, k, v, logf)
    ssm, h_re, h_im = s5_mixer(u, h0_re, h0_im, p["a_re"], p["a_im"], p["log_dt"], p["b_re"], p["b_im"],
                               p["c_re"], p["c_im"], p["d_skip"], p["w_glu"], p["b_glu"])
    mix = jnp.concatenate([rms_norm(attn, p["g_attn_out"]), rms_norm(ssm, p["g_ssm_out"])], axis=-1)
    mix = mix @ p["w_out"].astype(F32)
    x = x + gt_m[:, None, :] * rms_norm(mix, p["g_post_mix"])
    h = modulate(x, p["g_pre_ffn"], sh_f, sc_f)
    gu = h @ p["w_gate_up"].astype(F32)
    ffn = (jax.nn.silu(gu[..., :D_FF]) * gu[..., D_FF:]) @ p["w_down"].astype(F32)
    x = x + gt_f[:, None, :] * rms_norm(ffn, p["g_post_ffn"])
    return x, k, v, logf, h_re, h_im


def setup_inputs(seed: int = 0) -> dict:
    key = jax.random.key(seed)
    ks = jax.random.split(key, 40)
    n_pages = PAST_LEN // PAGE_SIZE
    n_used = DEC_BATCH * n_pages
    n_phys = n_used + n_used // 4
    nrm = jax.random.normal
    page_table = jax.random.permutation(ks[0], n_phys)[:n_used].reshape(DEC_BATCH, n_pages).astype(jnp.int32)
    a_im_init = math.pi * jnp.arange(SSM_STATE, dtype=F32)
    return {
        "x_prompt": nrm(ks[1], (BATCH, SEQ, D_MODEL), F32),
        "x_sample": nrm(ks[2], (DEC_BATCH, DEC_SEQ, D_MODEL), F32),
        "c_prompt": nrm(ks[3], (BATCH, D_MODEL), F32),
        "c_sample": nrm(ks[4], (DEC_BATCH, D_MODEL), F32),
        "cache_k": nrm(ks[5], (DEPTH, n_phys, PAGE_SIZE, N_HEADS, HEAD_DIM), F32),
        "cache_v": nrm(ks[6], (DEPTH, n_phys, PAGE_SIZE, N_HEADS, HEAD_DIM), F32),
        "cache_logf": jax.nn.log_sigmoid(3.0 + nrm(ks[7], (DEPTH, n_phys, PAGE_SIZE, N_HEADS), F32)),
        "state_ssm_re": 0.5 * nrm(ks[8], (DEPTH, DEC_BATCH, N_SSM_GROUPS, SSM_STATE), F32),
        "state_ssm_im": 0.5 * nrm(ks[9], (DEPTH, DEC_BATCH, N_SSM_GROUPS, SSM_STATE), F32),
        "page_table": page_table,
        "w_ada": 0.3 * D_MODEL ** -0.5 * nrm(ks[10], (DEPTH, D_MODEL, 6 * D_MODEL), F32),
        "b_ada": 0.01 * nrm(ks[11], (DEPTH, 6 * D_MODEL), F32),
        "g_pre_mix": 1.0 + 0.01 * nrm(ks[12], (DEPTH, D_MODEL), F32),
        "g_post_mix": 1.0 + 0.01 * nrm(ks[13], (DEPTH, D_MODEL), F32),
        "g_pre_ffn": 1.0 + 0.01 * nrm(ks[14], (DEPTH, D_MODEL), F32),
        "g_post_ffn": 1.0 + 0.01 * nrm(ks[15], (DEPTH, D_MODEL), F32),
        "w_in": D_MODEL ** -0.5 * nrm(ks[16], (DEPTH, D_MODEL, D_IN), F32),
        "b_f": 3.0 + 0.1 * nrm(ks[17], (DEPTH, N_HEADS), F32),
        "a_re": -0.5 + 0.01 * nrm(ks[18], (DEPTH, N_SSM_GROUPS, SSM_STATE), F32),
        "a_im": a_im_init + 0.01 * nrm(ks[19], (DEPTH, N_SSM_GROUPS, SSM_STATE), F32),
        "log_dt": jax.random.uniform(ks[20], (DEPTH, N_SSM_GROUPS), F32, math.log(DT_MIN), math.log(DT_MAX)),
        "b_re": (2 * SSM_GROUP) ** -0.5 * nrm(ks[21], (DEPTH, N_SSM_GROUPS, SSM_STATE, SSM_GROUP), F32),
        "b_im": (2 * SSM_GROUP) ** -0.5 * nrm(ks[22], (DEPTH, N_SSM_GROUPS, SSM_STATE, SSM_GROUP), F32),
        "c_re": (2 * SSM_STATE) ** -0.5 * nrm(ks[23], (DEPTH, N_SSM_GROUPS, SSM_GROUP, SSM_STATE), F32),
        "c_im": (2 * SSM_STATE) ** -0.5 * nrm(ks[24], (DEPTH, N_SSM_GROUPS, SSM_GROUP, SSM_STATE), F32),
        "d_skip": nrm(ks[25], (DEPTH, N_SSM_GROUPS, SSM_GROUP), F32),
        "w_glu": D_SSM ** -0.5 * nrm(ks[26], (DEPTH, D_SSM, D_SSM), F32),
        "b_glu": 0.01 * nrm(ks[27], (DEPTH, D_SSM), F32),
        "g_attn_out": 1.0 + 0.01 * nrm(ks[28], (DEPTH, D_ATTN), F32),
        "g_ssm_out": 1.0 + 0.01 * nrm(ks[29], (DEPTH, D_SSM), F32),
        "w_out": D_MIX ** -0.5 * nrm(ks[30], (DEPTH, D_MIX, D_MODEL), F32),
        "w_gate_up": D_MODEL ** -0.5 * nrm(ks[31], (DEPTH, D_MODEL, 2 * D_FF), F32),
        "w_down": D_FF ** -0.5 * nrm(ks[32], (DEPTH, D_FF, D_MODEL), F32),
    }


def reference(x_prompt, x_sample, c_prompt, c_sample, cache_k, cache_v, cache_logf, state_ssm_re, state_ssm_im,
              page_table, w_ada, b_ada, g_pre_mix, g_post_mix, g_pre_ffn, g_post_ffn, w_in, b_f, a_re, a_im,
              log_dt, b_re, b_im, c_re, c_im, d_skip, w_glu, b_glu, g_attn_out, g_ssm_out, w_out, w_gate_up, w_down):
    yp = x_prompt.astype(F32)
    ys = x_sample.astype(F32)
    B, S = x_prompt.shape[:2]
    kp_l, vp_l, fp_l, rp_l, ip_l = [], [], [], [], []
    ks_l, vs_l, fs_l, rs_l, is_l = [], [], [], [], []
    zero_state = jnp.zeros((B, N_SSM_GROUPS, SSM_STATE), F32)
    for l in range(DEPTH):
        p = dict(w_ada=w_ada[l], b_ada=b_ada[l], g_pre_mix=g_pre_mix[l], g_post_mix=g_post_mix[l],
                 g_pre_ffn=g_pre_ffn[l], g_post_ffn=g_post_ffn[l], w_in=w_in[l], b_f=b_f[l], a_re=a_re[l],
                 a_im=a_im[l], log_dt=log_dt[l], b_re=b_re[l], b_im=b_im[l], c_re=c_re[l], c_im=c_im[l],
                 d_skip=d_skip[l], w_glu=w_glu[l], b_glu=b_glu[l], g_attn_out=g_attn_out[l],
                 g_ssm_out=g_ssm_out[l], w_out=w_out[l], w_gate_up=w_gate_up[l], w_down=w_down[l])
        yp, kp, vp, fp, rp, ip = layer_forward(yp, c_prompt, fox_prompt, zero_state, zero_state, p)
        attend_s = functools.partial(fox_sample, cache_k=cache_k, cache_v=cache_v, cache_logf=cache_logf,
                                     page_table=page_table, layer=l)
        ys, kn, vn, fn, rn, im = layer_forward(ys, c_sample, attend_s, state_ssm_re[l], state_ssm_im[l], p)
        kp_l.append(kp.reshape(B, S // PAGE_SIZE, PAGE_SIZE, N_HEADS, HEAD_DIM))
        vp_l.append(vp.reshape(B, S // PAGE_SIZE, PAGE_SIZE, N_HEADS, HEAD_DIM))
        fp_l.append(fp.reshape(B, S // PAGE_SIZE, PAGE_SIZE, N_HEADS))
        rp_l.append(rp)
        ip_l.append(ip)
        ks_l.append(kn)
        vs_l.append(vn)
        fs_l.append(fn)
        rs_l.append(rn)
        is_l.append(im)
    return (yp.astype(x_prompt.dtype), ys.astype(x_sample.dtype),
            jnp.stack(kp_l), jnp.stack(vp_l), jnp.stack(fp_l), jnp.stack(rp_l), jnp.stack(ip_l),
            jnp.stack(ks_l), jnp.stack(vs_l), jnp.stack(fs_l), jnp.stack(rs_l), jnp.stack(is_l))
```

```python
import functools
import math

import numpy as np
import jax
import jax.numpy as jnp
from jax import lax
from jax.experimental import pallas as pl
from jax.experimental.pallas import tpu as pltpu

F32 = jnp.float32
BF16 = jnp.bfloat16
HI = lax.Precision.HIGHEST
EPS = 1e-6
NEG = -1e30
HEAD_DIM = 64
SSM_GROUP = 16
LANES = 128
GROUPS_PER_TILE = LANES // SSM_GROUP
VMEM_LIMIT = 56 * 1024 * 1024
SSM_CHUNK = 8
SQRT_2_OVER_PI = math.sqrt(2.0 / math.pi)


def _sigmoid(x):
    return 1.0 / (1.0 + jnp.exp(-x))


def _silu(x):
    return x * _sigmoid(x)


def _log_sigmoid(x):
    return jnp.minimum(x, 0.0) - jnp.log1p(jnp.exp(-jnp.abs(x)))


def _gelu_tanh(x):
    return x * (0.5 * (1.0 + jnp.tanh(SQRT_2_OVER_PI * (x + 0.044715 * (x * x * x)))))


def _rms(x):
    return x * lax.rsqrt(jnp.mean(x * x, axis=-1, keepdims=True) + EPS)


def _params(*sem):
    return pltpu.CompilerParams(dimension_semantics=sem, vmem_limit_bytes=VMEM_LIMIT)


def _mod_kernel(c_ref, w_ref, b_ref, o_ref):
    c = c_ref[...]
    o_ref[...] = jnp.dot(_silu(c), w_ref[...], precision=HI, preferred_element_type=F32) + b_ref[...]


def _mod_call(c_all, w_ada, b_ada):
    rows, d = c_all.shape
    n = w_ada.shape[1]
    return pl.pallas_call(
        _mod_kernel,
        grid=(n // d,),
        in_specs=[pl.BlockSpec((rows, d), lambda i: (0, 0)),
                  pl.BlockSpec((d, d), lambda i: (0, i)),
                  pl.BlockSpec((1, d), lambda i: (0, i))],
        out_specs=pl.BlockSpec((rows, d), lambda i: (0, i)),
        out_shape=jax.ShapeDtypeStruct((rows, n), F32),
        compiler_params=_params("arbitrary"),
        name="mod",
    )(c_all, w_ada, b_ada)


def _split3(f):
    hi = f.astype(BF16)
    r1 = f - hi.astype(F32)
    mid = r1.astype(BF16)
    lo = (r1 - mid.astype(F32)).astype(BF16)
    return hi, mid, lo


def _pre_prompt_kernel(x_ref, mod_ref, g_ref, wqkv_ref, wf_ref, bf_ref, wu_ref, tri_ref, pq_ref, pk_ref,
                       cq_ref, ck_ref, k_ref, v_ref, lf_ref, qa_ref, ka_ref, va_ref, u_ref, carry_ref,
                       *, n_heads, d_attn):
    @pl.when(pl.program_id(1) == 0)
    def _():
        carry_ref[...] = jnp.zeros_like(carry_ref)

    x = x_ref[0]
    tm = x.shape[0]
    h = _rms(x) * g_ref[...] * (1.0 + mod_ref[0, 1]) + mod_ref[0, 0]
    hb = h.astype(BF16)
    z = jnp.dot(hb, wqkv_ref[...], preferred_element_type=F32)
    u_ref[0] = jnp.dot(hb, wu_ref[...], preferred_element_type=F32)
    k_ref[0] = z[:, d_attn:2 * d_attn]
    v_ref[0] = z[:, 2 * d_attn:3 * d_attn]

    fl = jnp.dot(h, wf_ref[...], precision=HI, preferred_element_type=F32) + bf_ref[...]
    lane = lax.broadcasted_iota(jnp.int32, (tm, LANES), 1)
    logf = jnp.where(lane < n_heads, _log_sigmoid(fl), 0.0)
    lf_ref[0] = logf[:, :n_heads]
    cum = jnp.dot(tri_ref[...], logf, precision=HI, preferred_element_type=F32) + carry_ref[...]
    carry_ref[...] = cum[tm - 1:tm, :]

    fs = jnp.concatenate(_split3(cum), axis=1)
    augq = jnp.dot(fs, pq_ref[...], preferred_element_type=F32) + cq_ref[...]
    augk = jnp.dot(fs, pk_ref[...], preferred_element_type=F32) + ck_ref[...]
    low = lane < HEAD_DIM
    vone = jnp.where(lane == HEAD_DIM, 1.0, 0.0)
    scale = HEAD_DIM ** -0.5
    for j in range(n_heads // 2):
        zq = z[:, j * LANES:(j + 1) * LANES] * scale
        zk = z[:, d_attn + j * LANES:d_attn + (j + 1) * LANES]
        zv = z[:, 2 * d_attn + j * LANES:2 * d_attn + (j + 1) * LANES]
        for par in range(2):
            hh = 2 * j + par
            if par:
                zq, zk, zv = (pltpu.roll(a, HEAD_DIM, 1) for a in (zq, zk, zv))
            qa_ref[0, hh] = jnp.where(low, zq, augq[:, hh * LANES:(hh + 1) * LANES]).astype(BF16)
            ka_ref[0, hh] = jnp.where(low, zk, augk[:, hh * LANES:(hh + 1) * LANES]).astype(BF16)
            va_ref[0, hh] = jnp.where(low, zv, vone).astype(BF16)


def _aug_constants(n_heads):
    pq = np.zeros((3 * LANES, n_heads * LANES), np.float32)
    pk = np.zeros((3 * LANES, n_heads * LANES), np.float32)
    cq = np.zeros((1, n_heads * LANES), np.float32)
    ck = np.zeros((1, n_heads * LANES), np.float32)
    for h in range(n_heads):
        for piece in range(3):
            pq[piece * LANES + h, h * LANES + HEAD_DIM + piece] = 1.0
            pk[piece * LANES + h, h * LANES + HEAD_DIM + 3 + piece] = -1.0
            cq[0, h * LANES + HEAD_DIM + 3 + piece] = 1.0
            ck[0, h * LANES + HEAD_DIM + piece] = 1.0
    return jnp.asarray(pq, BF16), jnp.asarray(pk, BF16), jnp.asarray(cq), jnp.asarray(ck)


def _pre_prompt_call(x, mod_p, g_pre, w_qkv, w_f, b_f, w_u, n_heads, tm):
    b, s, d = x.shape
    d_attn = n_heads * HEAD_DIM
    d_ssm = w_u.shape[1]
    tri = jnp.asarray(np.tril(np.ones((tm, tm), np.float32)))
    pq, pk, cq, ck = _aug_constants(n_heads)
    const = lambda *shape: pl.BlockSpec(shape, lambda bi, ti: (0,) * len(shape))
    rows = lambda width: pl.BlockSpec((1, tm, width), lambda bi, ti: (bi, ti, 0))
    heads = pl.BlockSpec((1, n_heads, tm, LANES), lambda bi, ti: (bi, 0, ti, 0))
    aug_shape = jax.ShapeDtypeStruct((b, n_heads, s, LANES), BF16)
    return pl.pallas_call(
        functools.partial(_pre_prompt_kernel, n_heads=n_heads, d_attn=d_attn),
        grid=(b, s // tm),
        in_specs=[rows(d),
                  pl.BlockSpec((1, 6, 1, d), lambda bi, ti: (bi, 0, 0, 0)),
                  const(1, d), const(d, 3 * d_attn), const(d, LANES), const(1, LANES), const(d, d_ssm),
                  const(tm, tm), const(3 * LANES, n_heads * LANES), const(3 * LANES, n_heads * LANES),
                  const(1, n_heads * LANES), const(1, n_heads * LANES)],
        out_specs=[rows(d_attn), rows(d_attn), rows(n_heads), heads, heads, heads, rows(d_ssm)],
        out_shape=[jax.ShapeDtypeStruct((b, s, d_attn), F32), jax.ShapeDtypeStruct((b, s, d_attn), F32),
                   jax.ShapeDtypeStruct((b, s, n_heads), F32), aug_shape, aug_shape, aug_shape,
                   jax.ShapeDtypeStruct((b, s, d_ssm), F32)],
        scratch_shapes=[pltpu.VMEM((1, LANES), F32)],
        compiler_params=_params("arbitrary", "arbitrary"),
        name="pre_prompt",
    )(x, mod_p, g_pre, w_qkv, w_f, b_f, w_u, tri, pq, pk, cq, ck)


def _pre_sample_kernel(x_ref, mod_ref, g_ref, wqkv_ref, wf_ref, bf_ref, wu_ref, z_ref, lf_ref, u_ref):
    x = x_ref[...]
    h = _rms(x) * g_ref[...] * (1.0 + mod_ref[0, 1]) + mod_ref[0, 0]
    hb = h.astype(BF16)
    z_ref[...] = jnp.dot(hb, wqkv_ref[...], preferred_element_type=F32)
    u_ref[...] = jnp.dot(hb, wu_ref[...], preferred_element_type=F32)
    fl = jnp.dot(h, wf_ref[...], precision=HI, preferred_element_type=F32) + bf_ref[...]
    lf_ref[...] = _log_sigmoid(fl)


def _pre_sample_call(x, mod_s, g_pre, w_qkv, w_f, b_f, w_u):
    rows = x.shape[0]
    return pl.pallas_call(
        _pre_sample_kernel,
        out_shape=[jax.ShapeDtypeStruct((rows, w_qkv.shape[1]), F32),
                   jax.ShapeDtypeStruct((rows, LANES), F32),
                   jax.ShapeDtypeStruct((rows, w_u.shape[1]), F32)],
        compiler_params=pltpu.CompilerParams(vmem_limit_bytes=VMEM_LIMIT),
        name="pre_sample",
    )(x, mod_s, g_pre, w_qkv, w_f, b_f, w_u)


def _flash_kernel(q_ref, k_ref, v_ref, o_ref, *, tq):
    qi = pl.program_id(2)
    q = q_ref[0, 0]

    def step(ki, carry, masked):
        m, acc = carry
        off = pl.multiple_of(ki * tq, tq)
        k = k_ref[0, 0, pl.ds(off, tq), :]
        v = v_ref[0, 0, pl.ds(off, tq), :]
        s = lax.dot_general(q, k, (((1,), (1,)), ((), ())), preferred_element_type=F32)
        if masked:
            row = lax.broadcasted_iota(jnp.int32, s.shape, 0)
            col = lax.broadcasted_iota(jnp.int32, s.shape, 1)
            s = jnp.where(col <= row, s, NEG)
        m_new = jnp.maximum(m, jnp.max(s, axis=1, keepdims=True))
        p = jnp.exp(s - m_new)
        corr = jnp.exp(m - m_new)
        acc = acc * corr + jnp.dot(p.astype(BF16), v, preferred_element_type=F32)
        return m_new, acc

    init = (jnp.full((tq, 1), NEG, F32), jnp.zeros((tq, LANES), F32))
    carry = lax.fori_loop(0, qi, lambda ki, c: step(ki, c, False), init)
    _, acc = step(qi, carry, True)
    o = acc * (1.0 / acc[:, HEAD_DIM:HEAD_DIM + 1])
    lane = lax.broadcasted_iota(jnp.int32, o.shape, 1)
    o_ref[0, 0] = jnp.where(lane < HEAD_DIM, o, 0.0).astype(BF16)


def _flash_call(qa, ka, va, tq):
    b, h, s, _ = qa.shape
    qspec = pl.BlockSpec((1, 1, tq, LANES), lambda bi, hi, qi: (bi, hi, qi, 0))
    kvspec = pl.BlockSpec((1, 1, s, LANES), lambda bi, hi, qi: (bi, hi, 0, 0))
    return pl.pallas_call(
        functools.partial(_flash_kernel, tq=tq),
        grid=(b, h, s // tq),
        in_specs=[qspec, kvspec, kvspec],
        out_specs=qspec,
        out_shape=jax.ShapeDtypeStruct((b, h, s, LANES), BF16),
        compiler_params=_params("arbitrary", "arbitrary", "arbitrary"),
        name="flash",
    )(qa, ka, va)


def _ssm_discretize(a_re, a_im, log_dt, b_re, b_im, c_re, c_im):
    lam = lax.complex(a_re.astype(F32), a_im.astype(F32))
    dt = jnp.exp(log_dt.astype(F32))[:, None]
    lam_dt = lam * dt
    lam_bar = jnp.exp(lam_dt)
    b = lax.complex(b_re.astype(F32), b_im.astype(F32))
    b_bar = ((lam_bar - 1.0) / lam)[..., None] * b
    c = lax.complex(c_re.astype(F32), c_im.astype(F32))
    return lam_dt, lam_bar, b_bar, c


def _ssm_chunk_operators(lam_dt, b_bar, c, chunk):
    g, p = lam_dt.shape
    nt = g // GROUPS_PER_TILE
    steps = jnp.arange(chunk + 1, dtype=F32)
    pw = jnp.exp(lam_dt[None] * steps[:, None, None].astype(jnp.complex64))
    eye = jnp.eye(GROUPS_PER_TILE, dtype=F32)
    kmat = jnp.einsum("ghp,dgp,gpk->gdhk", c, pw[:chunk], b_bar).real
    kd = kmat.reshape(nt, GROUPS_PER_TILE, chunk, SSM_GROUP, SSM_GROUP)
    dblk = jnp.einsum("jgdhk,gf->jdgkfh", kd, eye).reshape(nt, chunk, LANES, LANES)
    lag = np.arange(chunk)[None, :] - np.arange(chunk)[:, None]
    tfull = dblk[:, np.clip(lag, 0, None)] * jnp.asarray(lag >= 0, F32)[None, :, :, None, None]
    toep = tfull.transpose(0, 1, 3, 2, 4).reshape(nt, chunk * LANES, chunk * LANES)
    wb = pw[chunk - 1 - np.arange(chunk)][..., None] * b_bar[None]
    wb = wb.reshape(chunk, nt, GROUPS_PER_TILE, p, SSM_GROUP)
    to_state = jnp.concatenate(
        [jnp.einsum("ljgpk,gf->jlgkfp", part, eye).reshape(nt, chunk * LANES, GROUPS_PER_TILE * p)
         for part in (wb.real, wb.imag)], axis=-1)
    cp = c[None] * pw[1:chunk + 1][:, :, None, :]
    cp = cp.reshape(chunk, nt, GROUPS_PER_TILE, SSM_GROUP, p)
    from_state = jnp.concatenate(
        [jnp.einsum("ljghp,gf->jgplfh", part, eye).reshape(nt, GROUPS_PER_TILE * p, chunk * LANES)
         for part in (cp.real, -cp.imag)], axis=1)
    decay = pw[chunk].reshape(nt, 1, GROUPS_PER_TILE * p)
    return toep.astype(BF16), to_state.astype(BF16), from_state.astype(BF16), decay.real, decay.imag


def _ssm_kernel(u_ref, t_ref, g_ref, c_ref, are_ref, aim_ref, d_ref, y_ref, hfin_ref, xin_ref, xs_ref, st_ref,
                *, chunk, n_chunks):
    r = pl.program_id(2)

    @pl.when(r == 0)
    def _():
        st_ref[...] = jnp.zeros_like(st_ref)

    half = st_ref.shape[1] // 2
    pieces = [u_ref[0, pl.ds(l, n_chunks, stride=chunk), :] for l in range(chunk)]
    u2 = jnp.concatenate([pc.astype(BF16) for pc in pieces], axis=1)
    xin_ref[...] = jnp.dot(u2, g_ref[0], preferred_element_type=F32)
    ar = are_ref[0]
    ai = aim_ref[0]

    def body(i, carry):
        xr, xi = carry
        base = pl.multiple_of(i * 8, 8)
        blk = xin_ref[pl.ds(base, 8), :]
        rows_r, rows_i = [], []
        for rr in range(8):
            rows_r.append(xr)
            rows_i.append(xi)
            xr, xi = (ar * xr - ai * xi + blk[rr:rr + 1, :half],
                      ar * xi + ai * xr + blk[rr:rr + 1, half:])
        xs_ref[pl.ds(base, 8), :] = jnp.concatenate(
            [jnp.concatenate(rows_r, axis=0), jnp.concatenate(rows_i, axis=0)], axis=1)
        return xr, xi

    xr, xi = lax.fori_loop(0, n_chunks // 8, body, (st_ref[:, :half], st_ref[:, half:]))
    st_ref[...] = jnp.concatenate([xr, xi], axis=1)
    y2 = (jnp.dot(u2, t_ref[0], preferred_element_type=F32)
          + jnp.dot(xs_ref[...].astype(BF16), c_ref[0], preferred_element_type=F32))
    dd = d_ref[0]
    for l in range(chunk):
        y_ref[0, pl.ds(l, n_chunks, stride=chunk), :] = y2[:, l * LANES:(l + 1) * LANES] + dd * pieces[l]

    @pl.when(r == pl.num_programs(2) - 1)
    def _():
        hfin_ref[0, 0] = st_ref[...]


def _ssm_call(u, toep, to_state, from_state, dec_re, dec_im, d_skip, rows_per_step):
    b, s, d_ssm = u.shape
    nt = d_ssm // LANES
    chunk = toep.shape[1] // LANES
    n_chunks = rows_per_step // chunk
    nstate = to_state.shape[2]
    tile = lambda *shape: pl.BlockSpec((1,) + shape, lambda j, bi, r: (j, 0, 0))
    useq = pl.BlockSpec((1, rows_per_step, LANES), lambda j, bi, r: (bi, r, j))
    return pl.pallas_call(
        functools.partial(_ssm_kernel, chunk=chunk, n_chunks=n_chunks),
        grid=(nt, b, s // rows_per_step),
        in_specs=[useq, tile(chunk * LANES, chunk * LANES), tile(chunk * LANES, nstate),
                  tile(nstate, chunk * LANES), tile(1, nstate // 2), tile(1, nstate // 2), tile(1, LANES)],
        out_specs=[useq, pl.BlockSpec((1, 1, 1, nstate), lambda j, bi, r: (j, bi, 0, 0))],
        out_shape=[jax.ShapeDtypeStruct((b, s, d_ssm), F32), jax.ShapeDtypeStruct((nt, b, 1, nstate), F32)],
        scratch_shapes=[pltpu.VMEM((n_chunks, nstate), F32), pltpu.VMEM((n_chunks, nstate), F32),
                        pltpu.VMEM((1, nstate), F32)],
        compiler_params=_params("arbitrary", "arbitrary", "arbitrary"),
        name="ssm",
    )(u, toep, to_state, from_state, dec_re, dec_im, d_skip)


def _ssm_step_kernel(u_ref, hre_ref, him_ref, bre_ref, bim_ref, lre_ref, lim_ref, cre_ref, cim_ref, d_ref,
                     y_ref, xre_ref, xim_ref):
    u = u_ref[...]
    hr, hi = hre_ref[...], him_ref[...]
    lr, li = lre_ref[...], lim_ref[...]
    xr = lr * hr - li * hi + jnp.dot(u, bre_ref[...], precision=HI, preferred_element_type=F32)
    xi = lr * hi + li * hr + jnp.dot(u, bim_ref[...], precision=HI, preferred_element_type=F32)
    xre_ref[...] = xr
    xim_ref[...] = xi
    y_ref[...] = (jnp.dot(xr, cre_ref[...], precision=HI, preferred_element_type=F32)
                  - jnp.dot(xi, cim_ref[...], precision=HI, preferred_element_type=F32) + d_ref[...] * u)


def _ssm_step_call(u, h_re, h_im, lam_bar, b_bar, c, d_skip):
    rows, d_ssm = u.shape
    g, p = lam_bar.shape
    eye = jnp.eye(g, dtype=F32)
    bmat = jnp.einsum("gpk,gf->gkfp", b_bar, eye.astype(jnp.complex64)).reshape(d_ssm, g * p)
    cmat = jnp.einsum("ghp,gf->gpfh", c, eye.astype(jnp.complex64)).reshape(g * p, d_ssm)
    lam_flat = lam_bar.reshape(1, g * p)
    return pl.pallas_call(
        _ssm_step_kernel,
        out_shape=[jax.ShapeDtypeStruct((rows, d_ssm), F32), jax.ShapeDtypeStruct((rows, g * p), F32),
                   jax.ShapeDtypeStruct((rows, g * p), F32)],
        compiler_params=pltpu.CompilerParams(vmem_limit_bytes=VMEM_LIMIT),
        name="ssm_step",
    )(u, h_re.reshape(rows, g * p), h_im.reshape(rows, g * p), bmat.real, bmat.imag,
      lam_flat.real, lam_flat.imag, cmat.real, cmat.imag, d_skip.reshape(1, d_ssm))


def _dec_kernel(pt_ref, qt_ref, kn_ref, vn_ref, lfn_ref, ck_ref, cv_ref, clf_ref, su_ref, o_ref,
                mrow_ref, mcol_ref, lrow_ref, lcol_ref, acc_ref, carry_ref, *, n_heads):
    del pt_ref
    pg = pl.program_id(1)
    qt = qt_ref[0] * (HEAD_DIM ** -0.5)
    eye = (lax.broadcasted_iota(jnp.int32, (n_heads, n_heads), 0)
           == lax.broadcasted_iota(jnp.int32, (n_heads, n_heads), 1))
    ones = jnp.ones((n_heads, HEAD_DIM), F32)

    def update(kp, vp, bias):
        t = bias.shape[0]
        s = jnp.dot(kp, qt, preferred_element_type=F32)
        s3 = s.reshape(t, n_heads, n_heads) + bias[:, None, :]
        s3 = jnp.where(eye[None], s3, NEG)
        smax = jnp.max(s3, axis=0)
        m_row = jnp.maximum(mrow_ref[...], jnp.max(smax, axis=0, keepdims=True))
        m_col = jnp.maximum(mcol_ref[...], jnp.max(smax, axis=1, keepdims=True))
        p3 = jnp.exp(s3 - m_row[None]).astype(BF16).astype(F32)
        psum = jnp.sum(p3, axis=0)
        corr_col = jnp.exp(mcol_ref[...] - m_col)
        lrow_ref[...] = lrow_ref[...] * jnp.exp(mrow_ref[...] - m_row) + jnp.sum(psum, axis=0, keepdims=True)
        lcol_ref[...] = lcol_ref[...] * corr_col + jnp.sum(psum, axis=1, keepdims=True)
        pb = jnp.dot(p3.reshape(t * n_heads, n_heads), ones, preferred_element_type=F32)
        acc_ref[...] = acc_ref[...] * corr_col + jnp.sum((pb * vp).reshape(t, n_heads, HEAD_DIM), axis=0)
        mrow_ref[...] = m_row
        mcol_ref[...] = m_col

    @pl.when(pg == 0)
    def _():
        mrow_ref[...] = jnp.full_like(mrow_ref, NEG)
        mcol_ref[...] = jnp.full_like(mcol_ref, NEG)
        lrow_ref[...] = jnp.zeros_like(lrow_ref)
        lcol_ref[...] = jnp.zeros_like(lcol_ref)
        acc_ref[...] = jnp.zeros_like(acc_ref)
        carry_ref[...] = lfn_ref[0]
        update(kn_ref[0], vn_ref[0], jnp.zeros((1, n_heads), F32))

    lf = clf_ref[0, 0]
    bias = jnp.dot(su_ref[...], lf, precision=HI, preferred_element_type=F32) + carry_ref[...]
    carry_ref[...] = carry_ref[...] + jnp.sum(lf, axis=0, keepdims=True)
    page = ck_ref.shape[2]
    update(ck_ref[0, 0].reshape(page * n_heads, HEAD_DIM), cv_ref[0, 0].reshape(page * n_heads, HEAD_DIM), bias)

    @pl.when(pg == pl.num_programs(1) - 1)
    def _():
        o_ref[0] = acc_ref[...] * (1.0 / lcol_ref[...])


def _dec_call(page_table, qt, kn, vn, lfn, cache_k, cache_v, cache_logf):
    db, n_pages = page_table.shape
    _, _, page, n_heads, _ = cache_k.shape
    su = jnp.asarray(np.triu(np.ones((page, page), np.float32), 1))
    pt_flat = page_table.reshape(-1)

    def phys(bi, pg, pt):
        return pt[bi * n_pages + (n_pages - 1 - pg)]

    per_b = lambda *shape: pl.BlockSpec((1,) + shape, lambda bi, pg, pt: (bi,) + (0,) * len(shape))
    grid_spec = pltpu.PrefetchScalarGridSpec(
        num_scalar_prefetch=1,
        grid=(db, n_pages),
        in_specs=[per_b(HEAD_DIM, n_heads), per_b(n_heads, HEAD_DIM), per_b(n_heads, HEAD_DIM), per_b(1, n_heads),
                  pl.BlockSpec((1, 1, page, n_heads, HEAD_DIM), lambda bi, pg, pt: (0, phys(bi, pg, pt), 0, 0, 0)),
                  pl.BlockSpec((1, 1, page, n_heads, HEAD_DIM), lambda bi, pg, pt: (0, phys(bi, pg, pt), 0, 0, 0)),
                  pl.BlockSpec((1, 1, page, n_heads), lambda bi, pg, pt: (0, phys(bi, pg, pt), 0, 0)),
                  pl.BlockSpec((page, page), lambda bi, pg, pt: (0, 0))],
        out_specs=per_b(n_heads, HEAD_DIM),
        scratch_shapes=[pltpu.VMEM((1, n_heads), F32), pltpu.VMEM((n_heads, 1), F32),
                        pltpu.VMEM((1, n_heads), F32), pltpu.VMEM((n_heads, 1), F32),
                        pltpu.VMEM((n_heads, HEAD_DIM), F32), pltpu.VMEM((1, n_heads), F32)])
    return pl.pallas_call(
        functools.partial(_dec_kernel, n_heads=n_heads),
        grid_spec=grid_spec,
        out_shape=jax.ShapeDtypeStruct((db, n_heads, HEAD_DIM), F32),
        compiler_params=_params("arbitrary", "arbitrary"),
        name="dec",
    )(pt_flat, qt, kn, vn, lfn, cache_k, cache_v, cache_logf, su)


def _post_kernel(x_ref, o_ref, ys_ref, mod_ref, wglu_ref, bglu_ref, gatt_ref, gssm_ref, wout_ref, gpm_ref,
                 gpf_ref, wgu_ref, wdn_ref, gpo_ref, y_ref, *, n_heads, ff_chunk):
    x = x_ref[0]
    pairs = []
    for j in range(n_heads // 2):
        even = o_ref[0, 2 * j].astype(F32)
        odd = o_ref[0, 2 * j + 1].astype(F32)
        pairs.append(even + pltpu.roll(odd, HEAD_DIM, 1))
    attn = jnp.concatenate(pairs, axis=1)
    gl = _gelu_tanh(ys_ref[0])
    gl = gl * _sigmoid(jnp.dot(gl.astype(BF16), wglu_ref[...], preferred_element_type=F32) + bglu_ref[...])
    mix = jnp.concatenate([_rms(attn) * gatt_ref[...], _rms(gl) * gssm_ref[...]], axis=1)
    mo = jnp.dot(mix.astype(BF16), wout_ref[...], preferred_element_type=F32)
    x1 = x + mod_ref[0, 2] * (_rms(mo) * gpm_ref[...])
    h2 = (_rms(x1) * gpf_ref[...] * (1.0 + mod_ref[0, 4]) + mod_ref[0, 3]).astype(BF16)
    d_ff = wdn_ref.shape[0]
    acc = jnp.zeros_like(x)
    for c0 in range(0, d_ff, ff_chunk):
        gate = jnp.dot(h2, wgu_ref[:, c0:c0 + ff_chunk], preferred_element_type=F32)
        up = jnp.dot(h2, wgu_ref[:, d_ff + c0:d_ff + c0 + ff_chunk], preferred_element_type=F32)
        acc = acc + jnp.dot((_silu(gate) * up).astype(BF16), wdn_ref[c0:c0 + ff_chunk, :],
                            preferred_element_type=F32)
    y_ref[0] = x1 + mod_ref[0, 5] * (_rms(acc) * gpo_ref[...])


def _post_call(x, o, ys, mod, w_glu, b_glu, g_attn, g_ssm, w_out, g_post_mix, g_pre_ffn, w_gate_up, w_down,
               g_post_ffn, tm, ff_chunk):
    nb, s, d = x.shape
    n_heads = o.shape[1]
    d_ssm = ys.shape[2]
    mod_rows = mod.shape[2]
    const = lambda a: pl.BlockSpec(a.shape, lambda bi, ti: (0,) * a.ndim, pipeline_mode=pl.Buffered(1))
    rows = lambda width: pl.BlockSpec((1, tm, width), lambda bi, ti: (bi, ti, 0))
    mod_spec = (pl.BlockSpec((1, 6, 1, d), lambda bi, ti: (bi, 0, 0, 0)) if mod_rows == 1
                else pl.BlockSpec((1, 6, tm, d), lambda bi, ti: (bi, 0, ti, 0)))
    weights = (w_glu, b_glu, g_attn, g_ssm, w_out, g_post_mix, g_pre_ffn, w_gate_up, w_down, g_post_ffn)
    return pl.pallas_call(
        functools.partial(_post_kernel, n_heads=n_heads, ff_chunk=ff_chunk),
        grid=(nb, s // tm),
        in_specs=[rows(d), pl.BlockSpec((1, n_heads, tm, LANES), lambda bi, ti: (bi, 0, ti, 0)), rows(d_ssm),
                  mod_spec] + [const(w) for w in weights],
        out_specs=rows(d),
        out_shape=jax.ShapeDtypeStruct((nb, s, d), F32),
        compiler_params=_params("arbitrary", "arbitrary"),
        name="post",
    )(x, o, ys, mod, *weights)


def _row(v):
    return v.reshape(1, -1).astype(F32)


def _layer(xp, xs, cp, cs, cache_k, cache_v, cache_logf, h_re, h_im, page_table, w):
    b, s, d = xp.shape
    db = xs.shape[0]
    n_heads = w["b_f"].shape[0]
    d_attn = n_heads * HEAD_DIM
    d_ff = w["w_down"].shape[0]
    g, p = w["a_re"].shape
    d_ssm = g * SSM_GROUP

    w_in = w["w_in"]
    w_qkv = w_in[:, :3 * d_attn].astype(BF16)
    w_f = jnp.pad(w_in[:, 3 * d_attn:3 * d_attn + n_heads].astype(F32), ((0, 0), (0, LANES - n_heads)))
    b_f = jnp.pad(_row(w["b_f"]), ((0, 0), (0, LANES - n_heads)))
    w_u = w_in[:, 3 * d_attn + n_heads:].astype(BF16)
    post_w = (w["w_glu"].astype(BF16), _row(w["b_glu"]), _row(w["g_attn_out"]), _row(w["g_ssm_out"]),
              w["w_out"].astype(BF16), _row(w["g_post_mix"]), _row(w["g_pre_ffn"]), w["w_gate_up"].astype(BF16),
              w["w_down"].astype(BF16), _row(w["g_post_ffn"]))
    g_pre = _row(w["g_pre_mix"])

    n_cond = b + db
    pad_rows = -n_cond % 8
    c_all = jnp.pad(jnp.concatenate([cp, cs], axis=0).astype(F32), ((0, pad_rows), (0, 0)))
    mod = _mod_call(c_all, w["w_ada"].astype(F32), _row(w["b_ada"]))
    mod_p = mod[:b].reshape(b, 6, 1, d)
    mod_s = mod[b:n_cond].reshape(db, 6, d).transpose(1, 0, 2)[None]

    tm = min(512, s)
    k_p, v_p, lf_p, qa, ka, va, u_p = _pre_prompt_call(xp, mod_p, g_pre, w_qkv, w_f, b_f, w_u, n_heads, tm)
    o_p = _flash_call(qa, ka, va, min(512, s))
    lam_dt, lam_bar, b_bar, c = _ssm_discretize(w["a_re"], w["a_im"], w["log_dt"], w["b_re"], w["b_im"],
                                                w["c_re"], w["c_im"])
    toep, to_state, from_state, dec_re, dec_im = _ssm_chunk_operators(lam_dt, b_bar, c, SSM_CHUNK)
    nt = d_ssm // LANES
    ys_p, hfin = _ssm_call(u_p, toep, to_state, from_state, dec_re, dec_im,
                           w["d_skip"].astype(F32).reshape(nt, 1, LANES), min(2048, s))
    y_p = _post_call(xp, o_p, ys_p, mod_p, *post_w, tm=tm, ff_chunk=d_ff // 2)
    half = hfin.shape[-1] // 2
    hre_p = hfin[:, :, 0, :half].transpose(1, 0, 2).reshape(b, g, p)
    him_p = hfin[:, :, 0, half:].transpose(1, 0, 2).reshape(b, g, p)

    xs2 = xs.reshape(db, d)
    z_s, lf_s, u_s = _pre_sample_call(xs2, mod_s, g_pre, w_qkv, w_f, b_f, w_u)
    q_s = z_s[:, :d_attn].reshape(db, n_heads, HEAD_DIM)
    k_s = z_s[:, d_attn:2 * d_attn].reshape(db, n_heads, HEAD_DIM)
    v_s = z_s[:, 2 * d_attn:].reshape(db, n_heads, HEAD_DIM)
    lfn = lf_s[:, :n_heads]
    o_s = _dec_call(page_table, q_s.transpose(0, 2, 1), k_s, v_s, lfn.reshape(db, 1, n_heads),
                    cache_k, cache_v, cache_logf)
    o_s = jnp.pad(o_s.transpose(1, 0, 2), ((0, 0), (0, 0), (0, LANES - HEAD_DIM))).astype(BF16)[None]
    ys_s, hre_s, him_s = _ssm_step_call(u_s, h_re, h_im, lam_bar, b_bar, c, w["d_skip"].astype(F32))
    y_s = _post_call(xs2[None], o_s, ys_s[None], mod_s, *post_w, tm=db, ff_chunk=d_ff // 2)

    n_pg = s // cache_k.shape[2]
    page = cache_k.shape[2]
    return dict(
        y_p=y_p, y_s=y_s.reshape(db, 1, d),
        k_p=k_p.reshape(b, n_pg, page, n_heads, HEAD_DIM), v_p=v_p.reshape(b, n_pg, page, n_heads, HEAD_DIM),
        f_p=lf_p.reshape(b, n_pg, page, n_heads), r_p=hre_p, i_p=him_p,
        k_s=k_s.reshape(db, 1, n_heads, HEAD_DIM), v_s=v_s.reshape(db, 1, n_heads, HEAD_DIM),
        f_s=lfn.reshape(db, 1, n_heads), r_s=hre_s.reshape(db, g, p), i_s=him_s.reshape(db, g, p))


def kernel(x_prompt, x_sample, c_prompt, c_sample, cache_k, cache_v, cache_logf, state_ssm_re, state_ssm_im,
           page_table, w_ada, b_ada, g_pre_mix, g_post_mix, g_pre_ffn, g_post_ffn, w_in, b_f, a_re, a_im,
           log_dt, b_re, b_im, c_re, c_im, d_skip, w_glu, b_glu, g_attn_out, g_ssm_out, w_out, w_gate_up, w_down):
    depth = w_in.shape[0]
    assert depth == 1 and x_sample.shape[1] == 1, "single layer, one new token per sequence"
    weights = dict(w_ada=w_ada, b_ada=b_ada, g_pre_mix=g_pre_mix, g_post_mix=g_post_mix, g_pre_ffn=g_pre_ffn,
                   g_post_ffn=g_post_ffn, w_in=w_in, b_f=b_f, a_re=a_re, a_im=a_im, log_dt=log_dt, b_re=b_re,
                   b_im=b_im, c_re=c_re, c_im=c_im, d_skip=d_skip, w_glu=w_glu, b_glu=b_glu,
                   g_attn_out=g_attn_out, g_ssm_out=g_ssm_out, w_out=w_out, w_gate_up=w_gate_up, w_down=w_down)
    w0 = {name: val[0] for name, val in weights.items()}
    r = _layer(x_prompt.astype(F32), x_sample.astype(F32), c_prompt, c_sample, cache_k, cache_v, cache_logf,
               state_ssm_re[0], state_ssm_im[0], page_table, w0)
    stack = lambda a: a[None]
    return (r["y_p"].astype(x_prompt.dtype), r["y_s"].astype(x_sample.dtype),
            stack(r["k_p"]), stack(r["v_p"]), stack(r["f_p"]), stack(r["r_p"]), stack(r["i_p"]),
            stack(r["k_s"]), stack(r["v_s"]), stack(r["f_s"]), stack(r["r_s"]), stack(r["i_s"]))
```

```python
import functools
import math

import numpy as np
import jax
import jax.numpy as jnp
from jax import lax
from jax.experimental import pallas as pl
from jax.experimental.pallas import tpu as pltpu

F32 = jnp.float32
BF16 = jnp.bfloat16
HI = lax.Precision.HIGHEST
EPS = 1e-6
NEG = -1e30
HEAD_DIM = 64
SSM_GROUP = 16
LANES = 128
GROUPS_PER_TILE = LANES // SSM_GROUP
VMEM_LIMIT = 56 * 1024 * 1024
SSM_CHUNK = 8
SQRT_2_OVER_PI = math.sqrt(2.0 / math.pi)


def _sigmoid(x):
    return 1.0 / (1.0 + jnp.exp(-x))


def _silu(x):
    return x * _sigmoid(x)


def _log_sigmoid(x):
    return jnp.minimum(x, 0.0) - jnp.log1p(jnp.exp(-jnp.abs(x)))


def _gelu_tanh(x):
    return x * (0.5 * (1.0 + jnp.tanh(SQRT_2_OVER_PI * (x + 0.044715 * (x * x * x)))))


def _rms(x):
    return x * lax.rsqrt(jnp.mean(x * x, axis=-1, keepdims=True) + EPS)


def _params(*sem):
    return pltpu.CompilerParams(dimension_semantics=sem, vmem_limit_bytes=VMEM_LIMIT)


def _mod_kernel(c_ref, w_ref, b_ref, o_ref):
    c = c_ref[...]
    o_ref[...] = jnp.dot(_silu(c), w_ref[...], precision=HI, preferred_element_type=F32) + b_ref[...]


def _mod_call(c_all, w_ada, b_ada):
    rows, d = c_all.shape
    n = w_ada.shape[1]
    return pl.pallas_call(
        _mod_kernel,
        grid=(n // d,),
        in_specs=[pl.BlockSpec((rows, d), lambda i: (0, 0)),
                  pl.BlockSpec((d, d), lambda i: (0, i)),
                  pl.BlockSpec((1, d), lambda i: (0, i))],
        out_specs=pl.BlockSpec((rows, d), lambda i: (0, i)),
        out_shape=jax.ShapeDtypeStruct((rows, n), F32),
        compiler_params=_params("arbitrary"),
        name="mod",
    )(c_all, w_ada, b_ada)


def _split3(f):
    hi = f.astype(BF16)
    r1 = f - hi.astype(F32)
    mid = r1.astype(BF16)
    lo = (r1 - mid.astype(F32)).astype(BF16)
    return hi, mid, lo


def _pre_prompt_kernel(x_ref, mod_ref, g_ref, wqkv_ref, wf_ref, bf_ref, wu_ref, tri_ref, pq_ref, pk_ref,
                       cq_ref, ck_ref, k_ref, v_ref, lf_ref, qa_ref, ka_ref, va_ref, u_ref, carry_ref,
                       *, n_heads, d_attn):
    @pl.when(pl.program_id(1) == 0)
    def _():
        carry_ref[...] = jnp.zeros_like(carry_ref)

    x = x_ref[0]
    tm = x.shape[0]
    h = _rms(x) * g_ref[...] * (1.0 + mod_ref[0, 1]) + mod_ref[0, 0]
    hb = h.astype(BF16)
    z = jnp.dot(hb, wqkv_ref[...], preferred_element_type=F32)
    u_ref[0] = jnp.dot(hb, wu_ref[...], preferred_element_type=F32)
    k_ref[0] = z[:, d_attn:2 * d_attn]
    v_ref[0] = z[:, 2 * d_attn:3 * d_attn]

    fl = jnp.dot(h, wf_ref[...], precision=HI, preferred_element_type=F32) + bf_ref[...]
    lane = lax.broadcasted_iota(jnp.int32, (tm, LANES), 1)
    logf = jnp.where(lane < n_heads, _log_sigmoid(fl), 0.0)
    lf_ref[0] = logf[:, :n_heads]
    cum = jnp.dot(tri_ref[...], logf, precision=HI, preferred_element_type=F32) + carry_ref[...]
    carry_ref[...] = cum[tm - 1:tm, :]

    fs = jnp.concatenate(_split3(cum), axis=1)
    augq = jnp.dot(fs, pq_ref[...], preferred_element_type=F32) + cq_ref[...]
    augk = jnp.dot(fs, pk_ref[...], preferred_element_type=F32) + ck_ref[...]
    low = lane < HEAD_DIM
    vone = jnp.where(lane == HEAD_DIM, 1.0, 0.0)
    scale = HEAD_DIM ** -0.5
    for j in range(n_heads // 2):
        zq = z[:, j * LANES:(j + 1) * LANES] * scale
        zk = z[:, d_attn + j * LANES:d_attn + (j + 1) * LANES]
        zv = z[:, 2 * d_attn + j * LANES:2 * d_attn + (j + 1) * LANES]
        for par in range(2):
            hh = 2 * j + par
            if par:
                zq, zk, zv = (pltpu.roll(a, HEAD_DIM, 1) for a in (zq, zk, zv))
            qa_ref[0, hh] = jnp.where(low, zq, augq[:, hh * LANES:(hh + 1) * LANES]).astype(BF16)
            ka_ref[0, hh] = jnp.where(low, zk, augk[:, hh * LANES:(hh + 1) * LANES]).astype(BF16)
            va_ref[0, hh, 0] = jnp.where(low, zv, vone).T.astype(BF16)


def _aug_constants(n_heads):
    pq = np.zeros((3 * LANES, n_heads * LANES), np.float32)
    pk = np.zeros((3 * LANES, n_heads * LANES), np.float32)
    cq = np.zeros((1, n_heads * LANES), np.float32)
    ck = np.zeros((1, n_heads * LANES), np.float32)
    for h in range(n_heads):
        for piece in range(3):
            pq[piece * LANES + h, h * LANES + HEAD_DIM + piece] = 1.0
            pk[piece * LANES + h, h * LANES + HEAD_DIM + 3 + piece] = -1.0
            cq[0, h * LANES + HEAD_DIM + 3 + piece] = 1.0
            ck[0, h * LANES + HEAD_DIM + piece] = 1.0
    return jnp.asarray(pq, BF16), jnp.asarray(pk, BF16), jnp.asarray(cq), jnp.asarray(ck)


def _pre_prompt_call(x, mod_p, g_pre, w_qkv, w_f, b_f, w_u, n_heads, tm):
    b, s, d = x.shape
    d_attn = n_heads * HEAD_DIM
    d_ssm = w_u.shape[1]
    tri = jnp.asarray(np.tril(np.ones((tm, tm), np.float32)))
    pq, pk, cq, ck = _aug_constants(n_heads)
    const = lambda *shape: pl.BlockSpec(shape, lambda bi, ti: (0,) * len(shape))
    rows = lambda width: pl.BlockSpec((1, tm, width), lambda bi, ti: (bi, ti, 0))
    heads = pl.BlockSpec((1, n_heads, tm, LANES), lambda bi, ti: (bi, 0, ti, 0))
    heads_t = pl.BlockSpec((1, n_heads, 1, LANES, tm), lambda bi, ti: (bi, 0, ti, 0, 0))
    aug_shape = jax.ShapeDtypeStruct((b, n_heads, s, LANES), BF16)
    aug_t_shape = jax.ShapeDtypeStruct((b, n_heads, s // tm, LANES, tm), BF16)
    return pl.pallas_call(
        functools.partial(_pre_prompt_kernel, n_heads=n_heads, d_attn=d_attn),
        grid=(b, s // tm),
        in_specs=[rows(d),
                  pl.BlockSpec((1, 6, 1, d), lambda bi, ti: (bi, 0, 0, 0)),
                  const(1, d), const(d, 3 * d_attn), const(d, LANES), const(1, LANES), const(d, d_ssm),
                  const(tm, tm), const(3 * LANES, n_heads * LANES), const(3 * LANES, n_heads * LANES),
                  const(1, n_heads * LANES), const(1, n_heads * LANES)],
        out_specs=[rows(d_attn), rows(d_attn), rows(n_heads), heads, heads, heads_t, rows(d_ssm)],
        out_shape=[jax.ShapeDtypeStruct((b, s, d_attn), F32), jax.ShapeDtypeStruct((b, s, d_attn), F32),
                   jax.ShapeDtypeStruct((b, s, n_heads), F32), aug_shape, aug_shape, aug_t_shape,
                   jax.ShapeDtypeStruct((b, s, d_ssm), F32)],
        scratch_shapes=[pltpu.VMEM((1, LANES), F32)],
        compiler_params=_params("arbitrary", "arbitrary"),
        name="pre_prompt",
    )(x, mod_p, g_pre, w_qkv, w_f, b_f, w_u, tri, pq, pk, cq, ck)


def _pre_sample_kernel(x_ref, mod_ref, g_ref, wqkv_ref, wf_ref, bf_ref, wu_ref, z_ref, lf_ref, u_ref):
    x = x_ref[...]
    h = _rms(x) * g_ref[...] * (1.0 + mod_ref[0, 1]) + mod_ref[0, 0]
    hb = h.astype(BF16)
    z_ref[...] = jnp.dot(hb, wqkv_ref[...], preferred_element_type=F32)
    u_ref[...] = jnp.dot(hb, wu_ref[...], preferred_element_type=F32)
    fl = jnp.dot(h, wf_ref[...], precision=HI, preferred_element_type=F32) + bf_ref[...]
    lf_ref[...] = _log_sigmoid(fl)


def _pre_sample_call(x, mod_s, g_pre, w_qkv, w_f, b_f, w_u):
    rows = x.shape[0]
    return pl.pallas_call(
        _pre_sample_kernel,
        out_shape=[jax.ShapeDtypeStruct((rows, w_qkv.shape[1]), F32),
                   jax.ShapeDtypeStruct((rows, LANES), F32),
                   jax.ShapeDtypeStruct((rows, w_u.shape[1]), F32)],
        compiler_params=pltpu.CompilerParams(vmem_limit_bytes=VMEM_LIMIT),
        name="pre_sample",
    )(x, mod_s, g_pre, w_qkv, w_f, b_f, w_u)


def _flash_kernel(q_ref, k_ref, vt_ref, o_ref, *, tq):
    qi = pl.program_id(2)
    q = q_ref[0, 0]

    def step(ki, carry, masked):
        m, acc = carry
        off = pl.multiple_of(ki * tq, tq)
        k = k_ref[0, 0, pl.ds(off, tq), :]
        st = lax.dot_general(k, q, (((1,), (1,)), ((), ())), preferred_element_type=F32)
        if masked:
            key = lax.broadcasted_iota(jnp.int32, st.shape, 0)
            qry = lax.broadcasted_iota(jnp.int32, st.shape, 1)
            st = jnp.where(key <= qry, st, NEG)
        m_new = jnp.maximum(m, jnp.max(st, axis=0, keepdims=True))
        pt = jnp.exp(st - m_new)
        corr = jnp.exp(m - m_new)
        acc = acc * corr + jnp.dot(vt_ref[0, 0, ki], pt.astype(BF16), preferred_element_type=F32)
        return m_new, acc

    init = (jnp.full((1, tq), NEG, F32), jnp.zeros((LANES, tq), F32))
    carry = lax.fori_loop(0, qi, lambda ki, c: step(ki, c, False), init)
    _, acc = step(qi, carry, True)
    o = (acc * (1.0 / acc[HEAD_DIM:HEAD_DIM + 1, :])).T
    lane = lax.broadcasted_iota(jnp.int32, o.shape, 1)
    o_ref[0, 0] = jnp.where(lane < HEAD_DIM, o, 0.0).astype(BF16)


def _flash_call(qa, ka, vat, tq):
    b, h, s, _ = qa.shape
    assert vat.shape[-1] == tq, "value chunks are laid out per key block"
    qspec = pl.BlockSpec((1, 1, tq, LANES), lambda bi, hi, qi: (bi, hi, qi, 0))
    kspec = pl.BlockSpec((1, 1, s, LANES), lambda bi, hi, qi: (bi, hi, 0, 0))
    vspec = pl.BlockSpec((1, 1, s // tq, LANES, tq), lambda bi, hi, qi: (bi, hi, 0, 0, 0))
    return pl.pallas_call(
        functools.partial(_flash_kernel, tq=tq),
        grid=(b, h, s // tq),
        in_specs=[qspec, kspec, vspec],
        out_specs=qspec,
        out_shape=jax.ShapeDtypeStruct((b, h, s, LANES), BF16),
        compiler_params=_params("arbitrary", "arbitrary", "arbitrary"),
        name="flash",
    )(qa, ka, vat)


def _cmul(ar, ai, br, bi):
    return ar * br - ai * bi, ar * bi + ai * br


def _ssm_discretize(a_re, a_im, log_dt, b_re, b_im, c_re, c_im):
    ar, ai = a_re.astype(F32), a_im.astype(F32)
    dt = jnp.exp(log_dt.astype(F32))[:, None]
    lam_dt = (ar * dt, ai * dt)
    mag = jnp.exp(lam_dt[0])
    lam_bar = (mag * jnp.cos(lam_dt[1]), mag * jnp.sin(lam_dt[1]))
    den = ar * ar + ai * ai
    nr, ni = lam_bar[0] - 1.0, lam_bar[1]
    coef = ((nr * ar + ni * ai) / den, (ni * ar - nr * ai) / den)
    b_bar = _cmul(coef[0][..., None], coef[1][..., None], b_re.astype(F32), b_im.astype(F32))
    c = (c_re.astype(F32), c_im.astype(F32))
    return lam_dt, lam_bar, b_bar, c


def _ssm_chunk_operators(lam_dt, b_bar, c, chunk):
    g, p = lam_dt[0].shape
    nt = g // GROUPS_PER_TILE
    steps = jnp.arange(chunk + 1, dtype=F32)[:, None, None]
    mag = jnp.exp(steps * lam_dt[0][None])
    pw = (mag * jnp.cos(steps * lam_dt[1][None]), mag * jnp.sin(steps * lam_dt[1][None]))
    eye = jnp.eye(GROUPS_PER_TILE, dtype=F32)
    pb = _cmul(pw[0][:chunk, :, :, None], pw[1][:chunk, :, :, None], b_bar[0][None], b_bar[1][None])
    kmat = (jnp.einsum("ghp,dgpk->gdhk", c[0], pb[0], precision=HI)
            - jnp.einsum("ghp,dgpk->gdhk", c[1], pb[1], precision=HI))
    kd = kmat.reshape(nt, GROUPS_PER_TILE, chunk, SSM_GROUP, SSM_GROUP)
    dblk = jnp.einsum("jgdhk,gf->jdgkfh", kd, eye, precision=HI).reshape(nt, chunk, LANES, LANES)
    lag = np.arange(chunk)[None, :] - np.arange(chunk)[:, None]
    tfull = dblk[:, np.clip(lag, 0, None)] * jnp.asarray(lag >= 0, F32)[None, :, :, None, None]
    toep = tfull.transpose(0, 1, 3, 2, 4).reshape(nt, chunk * LANES, chunk * LANES)
    rev = chunk - 1 - np.arange(chunk)
    wb = _cmul(pw[0][rev][..., None], pw[1][rev][..., None], b_bar[0][None], b_bar[1][None])
    to_state = jnp.concatenate(
        [jnp.einsum("ljgpk,gf->jlgkfp", part.reshape(chunk, nt, GROUPS_PER_TILE, p, SSM_GROUP), eye,
                    precision=HI).reshape(nt, chunk * LANES, GROUPS_PER_TILE * p) for part in wb], axis=-1)
    cp = _cmul(c[0][None], c[1][None], pw[0][1:chunk + 1][:, :, None, :], pw[1][1:chunk + 1][:, :, None, :])
    from_state = jnp.concatenate(
        [jnp.einsum("ljghp,gf->jgplfh", part.reshape(chunk, nt, GROUPS_PER_TILE, SSM_GROUP, p), eye,
                    precision=HI).reshape(nt, GROUPS_PER_TILE * p, chunk * LANES)
         for part in (cp[0], -cp[1])], axis=1)
    dec_re = pw[0][chunk].reshape(nt, 1, GROUPS_PER_TILE * p)
    dec_im = pw[1][chunk].reshape(nt, 1, GROUPS_PER_TILE * p)
    return toep.astype(BF16), to_state.astype(BF16), from_state.astype(BF16), dec_re, dec_im


def _ssm_kernel(u_ref, t_ref, g_ref, c_ref, are_ref, aim_ref, d_ref, y_ref, hfin_ref, xin_ref, xs_ref, st_ref,
                *, chunk, n_chunks):
    r = pl.program_id(2)

    @pl.when(r == 0)
    def _():
        st_ref[...] = jnp.zeros_like(st_ref)

    half = st_ref.shape[1] // 2
    pieces = [u_ref[0, pl.ds(l, n_chunks, stride=chunk), :] for l in range(chunk)]
    u2 = jnp.concatenate([pc.astype(BF16) for pc in pieces], axis=1)
    xin_ref[...] = jnp.dot(u2, g_ref[0], preferred_element_type=F32)
    ar = are_ref[0]
    ai = aim_ref[0]

    def body(i, carry):
        xr, xi = carry
        base = pl.multiple_of(i * 8, 8)
        blk = xin_ref[pl.ds(base, 8), :]
        rows_r, rows_i = [], []
        for rr in range(8):
            rows_r.append(xr)
            rows_i.append(xi)
            xr, xi = (ar * xr - ai * xi + blk[rr:rr + 1, :half],
                      ar * xi + ai * xr + blk[rr:rr + 1, half:])
        xs_ref[pl.ds(base, 8), :] = jnp.concatenate(
            [jnp.concatenate(rows_r, axis=0), jnp.concatenate(rows_i, axis=0)], axis=1)
        return xr, xi

    xr, xi = lax.fori_loop(0, n_chunks // 8, body, (st_ref[:, :half], st_ref[:, half:]))
    st_ref[...] = jnp.concatenate([xr, xi], axis=1)
    y2 = (jnp.dot(u2, t_ref[0], preferred_element_type=F32)
          + jnp.dot(xs_ref[...].astype(BF16), c_ref[0], preferred_element_type=F32))
    dd = d_ref[0]
    for l in range(chunk):
        y_ref[0, pl.ds(l, n_chunks, stride=chunk), :] = y2[:, l * LANES:(l + 1) * LANES] + dd * pieces[l]

    @pl.when(r == pl.num_programs(2) - 1)
    def _():
        hfin_ref[0, 0] = st_ref[...]


def _ssm_call(u, toep, to_state, from_state, dec_re, dec_im, d_skip, rows_per_step):
    b, s, d_ssm = u.shape
    nt = d_ssm // LANES
    chunk = toep.shape[1] // LANES
    n_chunks = rows_per_step // chunk
    nstate = to_state.shape[2]
    tile = lambda *shape: pl.BlockSpec((1,) + shape, lambda j, bi, r: (j, 0, 0))
    useq = pl.BlockSpec((1, rows_per_step, LANES), lambda j, bi, r: (bi, r, j))
    return pl.pallas_call(
        functools.partial(_ssm_kernel, chunk=chunk, n_chunks=n_chunks),
        grid=(nt, b, s // rows_per_step),
        in_specs=[useq, tile(chunk * LANES, chunk * LANES), tile(chunk * LANES, nstate),
                  tile(nstate, chunk * LANES), tile(1, nstate // 2), tile(1, nstate // 2), tile(1, LANES)],
        out_specs=[useq, pl.BlockSpec((1, 1, 1, nstate), lambda j, bi, r: (j, bi, 0, 0))],
        out_shape=[jax.ShapeDtypeStruct((b, s, d_ssm), F32), jax.ShapeDtypeStruct((nt, b, 1, nstate), F32)],
        scratch_shapes=[pltpu.VMEM((n_chunks, nstate), F32), pltpu.VMEM((n_chunks, nstate), F32),
                        pltpu.VMEM((1, nstate), F32)],
        compiler_params=_params("arbitrary", "arbitrary", "arbitrary"),
        name="ssm",
    )(u, toep, to_state, from_state, dec_re, dec_im, d_skip)


def _ssm_step_kernel(u_ref, hre_ref, him_ref, bre_ref, bim_ref, lre_ref, lim_ref, cre_ref, cim_ref, d_ref,
                     y_ref, xre_ref, xim_ref):
    u = u_ref[...]
    hr, hi = hre_ref[...], him_ref[...]
    lr, li = lre_ref[...], lim_ref[...]
    xr = lr * hr - li * hi + jnp.dot(u, bre_ref[...], precision=HI, preferred_element_type=F32)
    xi = lr * hi + li * hr + jnp.dot(u, bim_ref[...], precision=HI, preferred_element_type=F32)
    xre_ref[...] = xr
    xim_ref[...] = xi
    y_ref[...] = (jnp.dot(xr, cre_ref[...], precision=HI, preferred_element_type=F32)
                  - jnp.dot(xi, cim_ref[...], precision=HI, preferred_element_type=F32) + d_ref[...] * u)


def _ssm_step_call(u, h_re, h_im, lam_bar, b_bar, c, d_skip):
    rows, d_ssm = u.shape
    g, p = lam_bar[0].shape
    eye = jnp.eye(g, dtype=F32)
    bmat = [jnp.einsum("gpk,gf->gkfp", part, eye, precision=HI).reshape(d_ssm, g * p) for part in b_bar]
    cmat = [jnp.einsum("ghp,gf->gpfh", part, eye, precision=HI).reshape(g * p, d_ssm) for part in c]
    return pl.pallas_call(
        _ssm_step_kernel,
        out_shape=[jax.ShapeDtypeStruct((rows, d_ssm), F32), jax.ShapeDtypeStruct((rows, g * p), F32),
                   jax.ShapeDtypeStruct((rows, g * p), F32)],
        compiler_params=pltpu.CompilerParams(vmem_limit_bytes=VMEM_LIMIT),
        name="ssm_step",
    )(u, h_re.reshape(rows, g * p), h_im.reshape(rows, g * p), bmat[0], bmat[1],
      lam_bar[0].reshape(1, g * p), lam_bar[1].reshape(1, g * p), cmat[0], cmat[1], d_skip.reshape(1, d_ssm))


DEC_PAGES = 16


def _dec_kernel(pt_ref, q_ref, kn_ref, qt_ref, vnt_ref, lfn_ref, sli_ref, *rest, n_heads, pages):
    del pt_ref
    kt_refs, vt_refs, lf_refs = rest[:pages], rest[pages:2 * pages], rest[2 * pages:3 * pages]
    o_ref, qb_ref, m_ref, l_ref, acc_ref, carry_ref = rest[3 * pages:]
    step = pl.program_id(1)
    scale = HEAD_DIM ** -0.5

    @pl.when(step == 0)
    def _():
        qt = qt_ref[0] * scale
        vnt = vnt_ref[0]
        lane = lax.broadcasted_iota(jnp.int32, (HEAD_DIM, LANES), 1)
        for h in range(n_heads):
            qb_ref[h] = jnp.broadcast_to(qt[:, h:h + 1], (HEAD_DIM, LANES))
            acc_ref[h] = jnp.where(lane == 0, jnp.broadcast_to(vnt[:, h:h + 1], (HEAD_DIM, LANES)), 0.0)
        m_ref[...] = jnp.sum(q_ref[0] * kn_ref[0], axis=1, keepdims=True) * scale
        l_ref[...] = jnp.ones_like(l_ref)
        carry_ref[...] = lfn_ref[0]

    lf_all = jnp.concatenate([r[0, 0] for r in lf_refs], axis=0)
    sli = sli_ref[...]
    incl = sum(jnp.dot(piece, sli, preferred_element_type=F32) for piece in _split3(lf_all))
    run = carry_ref[...]
    scores = [None] * pages
    for i in reversed(range(pages)):
        rows = slice(i * n_heads, (i + 1) * n_heads)
        bias = incl[rows] - lf_all[rows] + run
        run = run + incl[rows][:, 0:1]
        qk = [jnp.sum(kt_refs[i][0, 0, h] * qb_ref[h], axis=0, keepdims=True) for h in range(n_heads)]
        scores[i] = jnp.concatenate(qk, axis=0) + bias
    carry_ref[...] = run

    m_old = m_ref[...]
    m_new = jnp.maximum(m_old, jnp.max(functools.reduce(jnp.maximum, scores), axis=1, keepdims=True))
    corr = jnp.exp(m_old - m_new)
    probs = [jnp.exp(s - m_new) for s in scores]
    l_ref[...] = l_ref[...] * corr + jnp.sum(functools.reduce(jnp.add, probs), axis=1, keepdims=True)
    m_ref[...] = m_new
    for h in range(n_heads):
        a = acc_ref[h] * corr[h:h + 1, :]
        for i in range(pages):
            a = a + vt_refs[i][0, 0, h] * probs[i][h:h + 1, :]
        acc_ref[h] = a

    @pl.when(step == pl.num_programs(1) - 1)
    def _():
        inv = 1.0 / l_ref[...]
        for h in range(n_heads):
            o_ref[0, h] = jnp.sum(acc_ref[h], axis=1, keepdims=True) * inv[h:h + 1, :]


def _dec_call(page_table, q, kn, qt, vnt, lfn, cache_k, cache_v, cache_logf, pages):
    db, n_pages = page_table.shape
    _, _, page, n_heads, _ = cache_k.shape
    assert page == LANES and n_pages % pages == 0
    kt = jnp.transpose(cache_k, (0, 1, 3, 4, 2))
    vt = jnp.transpose(cache_v, (0, 1, 3, 4, 2))
    lft = jnp.transpose(cache_logf, (0, 1, 3, 2))
    sli = jnp.asarray(np.tril(np.ones((page, page), np.float32)), BF16)
    pt_flat = page_table.reshape(-1)

    def phys(i):
        return lambda bi, st, pt: pt[bi * n_pages + n_pages - pages * (st + 1) + i]

    per_b = lambda *shape: pl.BlockSpec((1,) + shape, lambda bi, st, pt: (bi,) + (0,) * len(shape))
    kv_spec = lambda i: pl.BlockSpec((1, 1, n_heads, HEAD_DIM, page),
                                     lambda bi, st, pt, f=phys(i): (0, f(bi, st, pt), 0, 0, 0))
    lf_spec = lambda i: pl.BlockSpec((1, 1, n_heads, page), lambda bi, st, pt, f=phys(i): (0, f(bi, st, pt), 0, 0))
    grid_spec = pltpu.PrefetchScalarGridSpec(
        num_scalar_prefetch=1,
        grid=(db, n_pages // pages),
        in_specs=([per_b(n_heads, HEAD_DIM), per_b(n_heads, HEAD_DIM), per_b(HEAD_DIM, n_heads),
                   per_b(HEAD_DIM, n_heads), per_b(n_heads, 1), pl.BlockSpec((page, page), lambda bi, st, pt: (0, 0))]
                  + [kv_spec(i) for i in range(pages)] + [kv_spec(i) for i in range(pages)]
                  + [lf_spec(i) for i in range(pages)]),
        out_specs=per_b(n_heads, HEAD_DIM, 1),
        scratch_shapes=[pltpu.VMEM((n_heads, HEAD_DIM, page), F32), pltpu.VMEM((n_heads, 1), F32),
                        pltpu.VMEM((n_heads, 1), F32), pltpu.VMEM((n_heads, HEAD_DIM, page), F32),
                        pltpu.VMEM((n_heads, 1), F32)])
    return pl.pallas_call(
        functools.partial(_dec_kernel, n_heads=n_heads, pages=pages),
        grid_spec=grid_spec,
        out_shape=jax.ShapeDtypeStruct((db, n_heads, HEAD_DIM, 1), F32),
        compiler_params=_params("arbitrary", "arbitrary"),
        name="dec",
    )(pt_flat, q, kn, qt, vnt, lfn, sli, *([kt] * pages), *([vt] * pages), *([lft] * pages))


def _post_kernel(x_ref, o_ref, ys_ref, mod_ref, wglu_ref, bglu_ref, gatt_ref, gssm_ref, wout_ref, gpm_ref,
                 gpf_ref, wgu_ref, wdn_ref, gpo_ref, y_ref, *, n_heads, ff_chunk):
    x = x_ref[0]
    pairs = []
    for j in range(n_heads // 2):
        even = o_ref[0, 2 * j].astype(F32)
        odd = o_ref[0, 2 * j + 1].astype(F32)
        pairs.append(even + pltpu.roll(odd, HEAD_DIM, 1))
    attn = jnp.concatenate(pairs, axis=1)
    gl = _gelu_tanh(ys_ref[0])
    gl = gl * _sigmoid(jnp.dot(gl.astype(BF16), wglu_ref[...], preferred_element_type=F32) + bglu_ref[...])
    mix = jnp.concatenate([_rms(attn) * gatt_ref[...], _rms(gl) * gssm_ref[...]], axis=1)
    mo = jnp.dot(mix.astype(BF16), wout_ref[...], preferred_element_type=F32)
    x1 = x + mod_ref[0, 2] * (_rms(mo) * gpm_ref[...])
    h2 = (_rms(x1) * gpf_ref[...] * (1.0 + mod_ref[0, 4]) + mod_ref[0, 3]).astype(BF16)
    d_ff = wdn_ref.shape[0]
    acc = jnp.zeros_like(x)
    for c0 in range(0, d_ff, ff_chunk):
        gate = jnp.dot(h2, wgu_ref[:, c0:c0 + ff_chunk], preferred_element_type=F32)
        up = jnp.dot(h2, wgu_ref[:, d_ff + c0:d_ff + c0 + ff_chunk], preferred_element_type=F32)
        acc = acc + jnp.dot((_silu(gate) * up).astype(BF16), wdn_ref[c0:c0 + ff_chunk, :],
                            preferred_element_type=F32)
    y_ref[0] = x1 + mod_ref[0, 5] * (_rms(acc) * gpo_ref[...])


def _post_call(x, o, ys, mod, w_glu, b_glu, g_attn, g_ssm, w_out, g_post_mix, g_pre_ffn, w_gate_up, w_down,
               g_post_ffn, tm, ff_chunk):
    nb, s, d = x.shape
    n_heads = o.shape[1]
    d_ssm = ys.shape[2]
    mod_rows = mod.shape[2]
    const = lambda a: pl.BlockSpec(a.shape, lambda bi, ti: (0,) * a.ndim, pipeline_mode=pl.Buffered(1))
    rows = lambda width: pl.BlockSpec((1, tm, width), lambda bi, ti: (bi, ti, 0))
    mod_spec = (pl.BlockSpec((1, 6, 1, d), lambda bi, ti: (bi, 0, 0, 0)) if mod_rows == 1
                else pl.BlockSpec((1, 6, tm, d), lambda bi, ti: (bi, 0, ti, 0)))
    weights = (w_glu, b_glu, g_attn, g_ssm, w_out, g_post_mix, g_pre_ffn, w_gate_up, w_down, g_post_ffn)
    return pl.pallas_call(
        functools.partial(_post_kernel, n_heads=n_heads, ff_chunk=ff_chunk),
        grid=(nb, s // tm),
        in_specs=[rows(d), pl.BlockSpec((1, n_heads, tm, LANES), lambda bi, ti: (bi, 0, ti, 0)), rows(d_ssm),
                  mod_spec] + [const(w) for w in weights],
        out_specs=rows(d),
        out_shape=jax.ShapeDtypeStruct((nb, s, d), F32),
        compiler_params=_params("arbitrary", "arbitrary"),
        name="post",
    )(x, o, ys, mod, *weights)


def _row(v):
    return v.reshape(1, -1).astype(F32)


def _layer(xp, xs, cp, cs, cache_k, cache_v, cache_logf, h_re, h_im, page_table, w):
    b, s, d = xp.shape
    db = xs.shape[0]
    n_heads = w["b_f"].shape[0]
    d_attn = n_heads * HEAD_DIM
    d_ff = w["w_down"].shape[0]
    g, p = w["a_re"].shape
    d_ssm = g * SSM_GROUP

    w_in = w["w_in"]
    w_qkv = w_in[:, :3 * d_attn].astype(BF16)
    w_f = jnp.pad(w_in[:, 3 * d_attn:3 * d_attn + n_heads].astype(F32), ((0, 0), (0, LANES - n_heads)))
    b_f = jnp.pad(_row(w["b_f"]), ((0, 0), (0, LANES - n_heads)))
    w_u = w_in[:, 3 * d_attn + n_heads:].astype(BF16)
    post_w = (w["w_glu"].astype(BF16), _row(w["b_glu"]), _row(w["g_attn_out"]), _row(w["g_ssm_out"]),
              w["w_out"].astype(BF16), _row(w["g_post_mix"]), _row(w["g_pre_ffn"]), w["w_gate_up"].astype(BF16),
              w["w_down"].astype(BF16), _row(w["g_post_ffn"]))
    g_pre = _row(w["g_pre_mix"])

    n_cond = b + db
    pad_rows = -n_cond % 8
    c_all = jnp.pad(jnp.concatenate([cp, cs], axis=0).astype(F32), ((0, pad_rows), (0, 0)))
    mod = _mod_call(c_all, w["w_ada"].astype(F32), _row(w["b_ada"]))
    mod_p = mod[:b].reshape(b, 6, 1, d)
    mod_s = mod[b:n_cond].reshape(db, 6, d).transpose(1, 0, 2)[None]

    tm = min(512, s)
    k_p, v_p, lf_p, qa, ka, va, u_p = _pre_prompt_call(xp, mod_p, g_pre, w_qkv, w_f, b_f, w_u, n_heads, tm)
    o_p = _flash_call(qa, ka, va, min(512, s))
    lam_dt, lam_bar, b_bar, c = _ssm_discretize(w["a_re"], w["a_im"], w["log_dt"], w["b_re"], w["b_im"],
                                                w["c_re"], w["c_im"])
    toep, to_state, from_state, dec_re, dec_im = _ssm_chunk_operators(lam_dt, b_bar, c, SSM_CHUNK)
    nt = d_ssm // LANES
    ys_p, hfin = _ssm_call(u_p, toep, to_state, from_state, dec_re, dec_im,
                           w["d_skip"].astype(F32).reshape(nt, 1, LANES), min(2048, s))
    y_p = _post_call(xp, o_p, ys_p, mod_p, *post_w, tm=tm, ff_chunk=d_ff // 2)
    half = hfin.shape[-1] // 2
    hre_p = hfin[:, :, 0, :half].transpose(1, 0, 2).reshape(b, g, p)
    him_p = hfin[:, :, 0, half:].transpose(1, 0, 2).reshape(b, g, p)

    xs2 = xs.reshape(db, d)
    z_s, lf_s, u_s = _pre_sample_call(xs2, mod_s, g_pre, w_qkv, w_f, b_f, w_u)
    q_s = z_s[:, :d_attn].reshape(db, n_heads, HEAD_DIM)
    k_s = z_s[:, d_attn:2 * d_attn].reshape(db, n_heads, HEAD_DIM)
    v_s = z_s[:, 2 * d_attn:].reshape(db, n_heads, HEAD_DIM)
    lfn = lf_s[:, :n_heads]
    n_pages = page_table.shape[1]
    o_s = _dec_call(page_table, q_s, k_s, q_s.transpose(0, 2, 1), v_s.transpose(0, 2, 1),
                    lfn.reshape(db, n_heads, 1), cache_k, cache_v, cache_logf, math.gcd(DEC_PAGES, n_pages))
    o_s = o_s.reshape(db, n_heads, HEAD_DIM)
    o_s = jnp.pad(o_s.transpose(1, 0, 2), ((0, 0), (0, 0), (0, LANES - HEAD_DIM))).astype(BF16)[None]
    ys_s, hre_s, him_s = _ssm_step_call(u_s, h_re, h_im, lam_bar, b_bar, c, w["d_skip"].astype(F32))
    y_s = _post_call(xs2[None], o_s, ys_s[None], mod_s, *post_w, tm=db, ff_chunk=d_ff // 2)

    n_pg = s // cache_k.shape[2]
    page = cache_k.shape[2]
    return dict(
        y_p=y_p, y_s=y_s.reshape(db, 1, d),
        k_p=k_p.reshape(b, n_pg, page, n_heads, HEAD_DIM), v_p=v_p.reshape(b, n_pg, page, n_heads, HEAD_DIM),
        f_p=lf_p.reshape(b, n_pg, page, n_heads), r_p=hre_p, i_p=him_p,
        k_s=k_s.reshape(db, 1, n_heads, HEAD_DIM), v_s=v_s.reshape(db, 1, n_heads, HEAD_DIM),
        f_s=lfn.reshape(db, 1, n_heads), r_s=hre_s.reshape(db, g, p), i_s=him_s.reshape(db, g, p))


def kernel(x_prompt, x_sample, c_prompt, c_sample, cache_k, cache_v, cache_logf, state_ssm_re, state_ssm_im,
           page_table, w_ada, b_ada, g_pre_mix, g_post_mix, g_pre_ffn, g_post_ffn, w_in, b_f, a_re, a_im,
           log_dt, b_re, b_im, c_re, c_im, d_skip, w_glu, b_glu, g_attn_out, g_ssm_out, w_out, w_gate_up, w_down):
    depth = w_in.shape[0]
    assert depth == 1 and x_sample.shape[1] == 1, "single layer, one new token per sequence"
    weights = dict(w_ada=w_ada, b_ada=b_ada, g_pre_mix=g_pre_mix, g_post_mix=g_post_mix, g_pre_ffn=g_pre_ffn,
                   g_post_ffn=g_post_ffn, w_in=w_in, b_f=b_f, a_re=a_re, a_im=a_im, log_dt=log_dt, b_re=b_re,
                   b_im=b_im, c_re=c_re, c_im=c_im, d_skip=d_skip, w_glu=w_glu, b_glu=b_glu,
                   g_attn_out=g_attn_out, g_ssm_out=g_ssm_out, w_out=w_out, w_gate_up=w_gate_up, w_down=w_down)
    w0 = {name: val[0] for name, val in weights.items()}
    r = _layer(x_prompt.astype(F32), x_sample.astype(F32), c_prompt, c_sample, cache_k, cache_v, cache_logf,
               state_ssm_re[0], state_ssm_im[0], page_table, w0)
    stack = lambda a: a[None]
    return (r["y_p"].astype(x_prompt.dtype), r["y_s"].astype(x_sample.dtype),
            stack(r["k_p"]), stack(r["v_p"]), stack(r["f_p"]), stack(r["r_p"]), stack(r["i_p"]),
            stack(r["k_s"]), stack(r["v_s"]), stack(r["f_s"]), stack(r["r_s"]), stack(r["i_s"]))
```

```python
import functools
import math

import numpy as np
import jax
import jax.numpy as jnp
from jax import lax
from jax.experimental import pallas as pl
from jax.experimental.pallas import tpu as pltpu

F32 = jnp.float32
BF16 = jnp.bfloat16
HI = lax.Precision.HIGHEST
EPS = 1e-6
NEG = -1e30
HEAD_DIM = 64
SSM_GROUP = 16
LANES = 128
GROUPS_PER_TILE = LANES // SSM_GROUP
VMEM_LIMIT = 56 * 1024 * 1024
SSM_CHUNK = 8
V_ROWS = 80
SQRT_2_OVER_PI = math.sqrt(2.0 / math.pi)
LOG2_E = math.log2(math.e)


def _sigmoid(x):
    return 1.0 / (1.0 + jnp.exp(-x))


def _silu(x):
    return x * _sigmoid(x)


def _log_sigmoid(x):
    return jnp.minimum(x, 0.0) - jnp.log1p(jnp.exp(-jnp.abs(x)))


def _gelu_tanh(x):
    return x * (0.5 * (1.0 + jnp.tanh(SQRT_2_OVER_PI * (x + 0.044715 * (x * x * x)))))


def _rms(x):
    return x * lax.rsqrt(jnp.mean(x * x, axis=-1, keepdims=True) + EPS)


def _params(*sem):
    return pltpu.CompilerParams(dimension_semantics=sem, vmem_limit_bytes=VMEM_LIMIT)


def _mod_kernel(c_ref, w_ref, b_ref, o_ref):
    c = c_ref[...]
    o_ref[...] = jnp.dot(_silu(c), w_ref[...], precision=HI, preferred_element_type=F32) + b_ref[...]


def _mod_call(c_all, w_ada, b_ada):
    rows, d = c_all.shape
    n = w_ada.shape[1]
    return pl.pallas_call(
        _mod_kernel,
        grid=(n // d,),
        in_specs=[pl.BlockSpec((rows, d), lambda i: (0, 0)),
                  pl.BlockSpec((d, d), lambda i: (0, i)),
                  pl.BlockSpec((1, d), lambda i: (0, i))],
        out_specs=pl.BlockSpec((rows, d), lambda i: (0, i)),
        out_shape=jax.ShapeDtypeStruct((rows, n), F32),
        compiler_params=_params("arbitrary"),
        name="mod",
    )(c_all, w_ada, b_ada)


def _split3(f):
    hi = f.astype(BF16)
    r1 = f - hi.astype(F32)
    mid = r1.astype(BF16)
    lo = (r1 - mid.astype(F32)).astype(BF16)
    return hi, mid, lo


def _pre_prompt_kernel(x_ref, mod_ref, g_ref, wqkv_ref, wfh_ref, wfl_ref, bf_ref, wu_ref, tri_ref, pq_ref, pk_ref,
                       cq_ref, ck_ref, kt_ref, vt_ref, lft_ref, qa_ref, ka_ref, va_ref, u_ref, carry_ref,
                       *, n_heads, d_attn):
    @pl.when(pl.program_id(1) == 0)
    def _():
        carry_ref[...] = jnp.zeros_like(carry_ref)

    x = x_ref[0]
    tm = x.shape[0]
    h = _rms(x) * g_ref[...] * (1.0 + mod_ref[0, 1]) + mod_ref[0, 0]
    hb = h.astype(BF16)
    z = jnp.dot(hb, wqkv_ref[...], preferred_element_type=F32)
    u_ref[0] = jnp.dot(hb, wu_ref[...], preferred_element_type=F32)

    h_lo = (h - hb.astype(F32)).astype(BF16)
    fl = (jnp.dot(hb, wfh_ref[...], preferred_element_type=F32) + jnp.dot(hb, wfl_ref[...], preferred_element_type=F32)
          + jnp.dot(h_lo, wfh_ref[...], preferred_element_type=F32) + bf_ref[...])
    lane = lax.broadcasted_iota(jnp.int32, (tm, LANES), 1)
    logf = jnp.where(lane < n_heads, _log_sigmoid(fl), 0.0)
    logf_t = logf.T
    for pg in range(tm // LANES):
        lft_ref[0, pg] = logf_t[:n_heads, pg * LANES:(pg + 1) * LANES]
    tri_sum = jnp.dot(tri_ref[...], jnp.concatenate(_split3(logf), axis=1), preferred_element_type=F32)
    cum = tri_sum[:, :LANES] + tri_sum[:, LANES:2 * LANES] + tri_sum[:, 2 * LANES:] + carry_ref[...]
    carry_ref[...] = cum[tm - 1:tm, :]

    fs = jnp.concatenate(_split3(cum * LOG2_E), axis=1)
    augq = jnp.dot(fs, pq_ref[...], preferred_element_type=F32) + cq_ref[...]
    augk = jnp.dot(fs, pk_ref[...], preferred_element_type=F32) + ck_ref[...]
    low = lane < HEAD_DIM
    vone = jnp.where(lane == HEAD_DIM, 1.0, 0.0)
    scale = HEAD_DIM ** -0.5 * LOG2_E
    for j in range(n_heads // 2):
        zq = z[:, j * LANES:(j + 1) * LANES] * scale
        zk = z[:, d_attn + j * LANES:d_attn + (j + 1) * LANES]
        zv = z[:, 2 * d_attn + j * LANES:2 * d_attn + (j + 1) * LANES]
        for par in range(2):
            hh = 2 * j + par
            if par:
                zq, zk, zv = (pltpu.roll(a, HEAD_DIM, 1) for a in (zq, zk, zv))
            qa_ref[0, hh] = jnp.where(low, zq, augq[:, hh * LANES:(hh + 1) * LANES]).astype(BF16)
            ka_ref[0, hh] = jnp.where(low, zk, augk[:, hh * LANES:(hh + 1) * LANES]).astype(BF16)
            k_t = zk.T
            v_t = jnp.where(low, zv, vone).T
            va_ref[0, hh, 0] = v_t[:V_ROWS].astype(BF16)
            for pg in range(tm // LANES):
                kt_ref[0, pg, hh] = k_t[:HEAD_DIM, pg * LANES:(pg + 1) * LANES]
                vt_ref[0, pg, hh] = v_t[:HEAD_DIM, pg * LANES:(pg + 1) * LANES]


def _aug_constants(n_heads):
    pq = np.zeros((3 * LANES, n_heads * LANES), np.float32)
    pk = np.zeros((3 * LANES, n_heads * LANES), np.float32)
    cq = np.zeros((1, n_heads * LANES), np.float32)
    ck = np.zeros((1, n_heads * LANES), np.float32)
    for h in range(n_heads):
        for piece in range(3):
            pq[piece * LANES + h, h * LANES + HEAD_DIM + piece] = 1.0
            pk[piece * LANES + h, h * LANES + HEAD_DIM + 3 + piece] = -1.0
            cq[0, h * LANES + HEAD_DIM + 3 + piece] = 1.0
            ck[0, h * LANES + HEAD_DIM + piece] = 1.0
    return jnp.asarray(pq, BF16), jnp.asarray(pk, BF16), jnp.asarray(cq), jnp.asarray(ck)


def _pre_prompt_call(x, mod_p, g_pre, w_qkv, w_f, b_f, w_u, n_heads, tm):
    b, s, d = x.shape
    d_attn = n_heads * HEAD_DIM
    d_ssm = w_u.shape[1]
    tri = jnp.asarray(np.tril(np.ones((tm, tm), np.float32)), BF16)
    w_f_hi = w_f.astype(BF16)
    w_f_lo = (w_f - w_f_hi.astype(F32)).astype(BF16)
    pq, pk, cq, ck = _aug_constants(n_heads)
    pages = tm // LANES
    const = lambda *shape: pl.BlockSpec(shape, lambda bi, ti: (0,) * len(shape))
    rows = lambda width: pl.BlockSpec((1, tm, width), lambda bi, ti: (bi, ti, 0))
    heads = pl.BlockSpec((1, n_heads, tm, LANES), lambda bi, ti: (bi, 0, ti, 0))
    heads_t = pl.BlockSpec((1, n_heads, 1, V_ROWS, tm), lambda bi, ti: (bi, 0, ti, 0, 0))
    paged = pl.BlockSpec((1, pages, n_heads, HEAD_DIM, LANES), lambda bi, ti: (bi, ti, 0, 0, 0))
    paged_shape = jax.ShapeDtypeStruct((b, s // LANES, n_heads, HEAD_DIM, LANES), F32)
    aug_shape = jax.ShapeDtypeStruct((b, n_heads, s, LANES), BF16)
    aug_t_shape = jax.ShapeDtypeStruct((b, n_heads, s // tm, V_ROWS, tm), BF16)
    return pl.pallas_call(
        functools.partial(_pre_prompt_kernel, n_heads=n_heads, d_attn=d_attn),
        grid=(b, s // tm),
        in_specs=[rows(d),
                  pl.BlockSpec((1, 6, 1, d), lambda bi, ti: (bi, 0, 0, 0)),
                  const(1, d), const(d, 3 * d_attn), const(d, LANES), const(d, LANES), const(1, LANES),
                  const(d, d_ssm), const(tm, tm), const(3 * LANES, n_heads * LANES),
                  const(3 * LANES, n_heads * LANES), const(1, n_heads * LANES), const(1, n_heads * LANES)],
        out_specs=[paged, paged, pl.BlockSpec((1, pages, n_heads, LANES), lambda bi, ti: (bi, ti, 0, 0)),
                   heads, heads, heads_t, rows(d_ssm)],
        out_shape=[paged_shape, paged_shape, jax.ShapeDtypeStruct((b, s // LANES, n_heads, LANES), F32),
                   aug_shape, aug_shape, aug_t_shape, jax.ShapeDtypeStruct((b, s, d_ssm), F32)],
        scratch_shapes=[pltpu.VMEM((1, LANES), F32)],
        compiler_params=_params("arbitrary", "arbitrary"),
        name="pre_prompt",
    )(x, mod_p, g_pre, w_qkv, w_f_hi, w_f_lo, b_f, w_u, tri, pq, pk, cq, ck)


def _pre_sample_kernel(x_ref, mod_ref, g_ref, wqkv_ref, wf_ref, bf_ref, wu_ref, z_ref, lf_ref, u_ref):
    x = x_ref[...]
    h = _rms(x) * g_ref[...] * (1.0 + mod_ref[0, 1]) + mod_ref[0, 0]
    hb = h.astype(BF16)
    z_ref[...] = jnp.dot(hb, wqkv_ref[...], preferred_element_type=F32)
    u_ref[...] = jnp.dot(hb, wu_ref[...], preferred_element_type=F32)
    fl = jnp.dot(h, wf_ref[...], precision=HI, preferred_element_type=F32) + bf_ref[...]
    lf_ref[...] = _log_sigmoid(fl)


def _pre_sample_call(x, mod_s, g_pre, w_qkv, w_f, b_f, w_u):
    rows = x.shape[0]
    return pl.pallas_call(
        _pre_sample_kernel,
        out_shape=[jax.ShapeDtypeStruct((rows, w_qkv.shape[1]), F32),
                   jax.ShapeDtypeStruct((rows, LANES), F32),
                   jax.ShapeDtypeStruct((rows, w_u.shape[1]), F32)],
        compiler_params=pltpu.CompilerParams(vmem_limit_bytes=VMEM_LIMIT),
        name="pre_sample",
    )(x, mod_s, g_pre, w_qkv, w_f, b_f, w_u)


def _flash_kernel(q_ref, k_ref, vt_ref, o_ref, st0_ref, st1_ref, m_ref, acc_ref, *, tq):
    qi = pl.program_id(2)
    q = q_ref[0, 0]

    def put_scores(kb, slot):
        off = pl.multiple_of(kb * tq, tq)
        slot[...] = lax.dot_general(k_ref[0, 0, pl.ds(off, tq), :], q, (((1,), (1,)), ((), ())),
                                    preferred_element_type=F32)

    def update(kb, slot, diagonal):
        st = slot[...]
        if diagonal:
            key = lax.broadcasted_iota(jnp.int32, st.shape, 0)
            qry = lax.broadcasted_iota(jnp.int32, st.shape, 1)
            st = jnp.where(key <= qry, st, NEG)
        m = m_ref[...]
        m_new = jnp.maximum(m, jnp.max(st, axis=0, keepdims=True))
        acc_ref[...] = acc_ref[...] * jnp.exp2(m - m_new) + jnp.dot(
            vt_ref[0, 0, kb], jnp.exp2(st - m_new).astype(BF16), preferred_element_type=F32)
        m_ref[...] = m_new

    def finish():
        acc = acc_ref[...]
        o = acc[:HEAD_DIM] * (1.0 / acc[HEAD_DIM:HEAD_DIM + 1, :])
        o_ref[0, 0] = jnp.concatenate([o, jnp.zeros((LANES - HEAD_DIM, tq), F32)], axis=0).T.astype(BF16)

    m_ref[...] = jnp.full_like(m_ref, NEG)
    acc_ref[...] = jnp.zeros_like(acc_ref)
    put_scores(0, st0_ref)

    def pair(j, carry):
        put_scores(2 * j + 1, st1_ref)
        update(2 * j, st0_ref, False)
        put_scores(2 * j + 2, st0_ref)
        update(2 * j + 1, st1_ref, False)
        return carry

    lax.fori_loop(0, qi // 2, pair, 0)

    @pl.when(qi % 2 == 1)
    def _():
        put_scores(qi, st1_ref)
        update(qi - 1, st0_ref, False)
        update(qi, st1_ref, True)
        finish()

    @pl.when(qi % 2 == 0)
    def _():
        update(qi, st0_ref, True)
        finish()


def _flash_call(qa, ka, vat, tq):
    b, h, s, _ = qa.shape
    assert vat.shape[-1] == tq, "value chunks are laid out per key block"
    qspec = pl.BlockSpec((1, 1, tq, LANES), lambda bi, hi, qi: (bi, hi, qi, 0))
    kspec = pl.BlockSpec((1, 1, s, LANES), lambda bi, hi, qi: (bi, hi, 0, 0))
    vspec = pl.BlockSpec((1, 1, s // tq, V_ROWS, tq), lambda bi, hi, qi: (bi, hi, 0, 0, 0))
    return pl.pallas_call(
        functools.partial(_flash_kernel, tq=tq),
        grid=(b, h, s // tq),
        in_specs=[qspec, kspec, vspec],
        out_specs=qspec,
        out_shape=jax.ShapeDtypeStruct((b, h, s, LANES), BF16),
        scratch_shapes=[pltpu.VMEM((tq, tq), F32), pltpu.VMEM((tq, tq), F32), pltpu.VMEM((1, tq), F32),
                        pltpu.VMEM((V_ROWS, tq), F32)],
        compiler_params=_params("arbitrary", "arbitrary", "arbitrary"),
        name="flash",
    )(qa, ka, vat)


def _cmul(ar, ai, br, bi):
    return ar * br - ai * bi, ar * bi + ai * br


def _ssm_discretize(a_re, a_im, log_dt, b_re, b_im, c_re, c_im):
    ar, ai = a_re.astype(F32), a_im.astype(F32)
    dt = jnp.exp(log_dt.astype(F32))[:, None]
    lam_dt = (ar * dt, ai * dt)
    mag = jnp.exp(lam_dt[0])
    lam_bar = (mag * jnp.cos(lam_dt[1]), mag * jnp.sin(lam_dt[1]))
    den = ar * ar + ai * ai
    nr, ni = lam_bar[0] - 1.0, lam_bar[1]
    coef = ((nr * ar + ni * ai) / den, (ni * ar - nr * ai) / den)
    b_bar = _cmul(coef[0][..., None], coef[1][..., None], b_re.astype(F32), b_im.astype(F32))
    c = (c_re.astype(F32), c_im.astype(F32))
    return lam_dt, lam_bar, b_bar, c


def _ssm_chunk_operators(lam_dt, b_bar, c, chunk):
    g, p = lam_dt[0].shape
    nt = g // GROUPS_PER_TILE
    steps = jnp.arange(chunk + 1, dtype=F32)[:, None, None]
    mag = jnp.exp(steps * lam_dt[0][None])
    pw = (mag * jnp.cos(steps * lam_dt[1][None]), mag * jnp.sin(steps * lam_dt[1][None]))
    eye = jnp.eye(GROUPS_PER_TILE, dtype=F32)
    pb = _cmul(pw[0][:chunk, :, :, None], pw[1][:chunk, :, :, None], b_bar[0][None], b_bar[1][None])
    kmat = (jnp.einsum("ghp,dgpk->gdhk", c[0], pb[0], precision=HI)
            - jnp.einsum("ghp,dgpk->gdhk", c[1], pb[1], precision=HI))
    kd = kmat.reshape(nt, GROUPS_PER_TILE, chunk, SSM_GROUP, SSM_GROUP)
    dblk = jnp.einsum("jgdhk,gf->jdgkfh", kd, eye, precision=HI).reshape(nt, chunk, LANES, LANES)
    lag = np.arange(chunk)[None, :] - np.arange(chunk)[:, None]
    tfull = dblk[:, np.clip(lag, 0, None)] * jnp.asarray(lag >= 0, F32)[None, :, :, None, None]
    toep = tfull.transpose(0, 1, 3, 2, 4).reshape(nt, chunk * LANES, chunk * LANES)
    rev = chunk - 1 - np.arange(chunk)
    wb = _cmul(pw[0][rev][..., None], pw[1][rev][..., None], b_bar[0][None], b_bar[1][None])
    to_state = jnp.concatenate(
        [jnp.einsum("ljgpk,gf->jlgkfp", part.reshape(chunk, nt, GROUPS_PER_TILE, p, SSM_GROUP), eye,
                    precision=HI).reshape(nt, chunk * LANES, GROUPS_PER_TILE * p) for part in wb], axis=-1)
    cp = _cmul(c[0][None], c[1][None], pw[0][1:chunk + 1][:, :, None, :], pw[1][1:chunk + 1][:, :, None, :])
    from_state = jnp.concatenate(
        [jnp.einsum("ljghp,gf->jgplfh", part.reshape(chunk, nt, GROUPS_PER_TILE, SSM_GROUP, p), eye,
                    precision=HI).reshape(nt, GROUPS_PER_TILE * p, chunk * LANES)
         for part in (cp[0], -cp[1])], axis=1)
    dec_re = pw[0][chunk].reshape(nt, 1, GROUPS_PER_TILE * p)
    dec_im = pw[1][chunk].reshape(nt, 1, GROUPS_PER_TILE * p)
    return toep.astype(BF16), to_state.astype(BF16), from_state.astype(BF16), dec_re, dec_im


def _ssm_kernel(u_ref, t_ref, g_ref, c_ref, are_ref, aim_ref, d_ref, y_ref, hfin_ref, xin_ref, xs_ref, st_ref,
                *, chunk, n_chunks):
    r = pl.program_id(2)

    @pl.when(r == 0)
    def _():
        st_ref[...] = jnp.zeros_like(st_ref)

    half = st_ref.shape[1] // 2
    pieces = [u_ref[0, pl.ds(l, n_chunks, stride=chunk), :] for l in range(chunk)]
    u2 = jnp.concatenate([pc.astype(BF16) for pc in pieces], axis=1)
    xin_ref[...] = jnp.dot(u2, g_ref[0], preferred_element_type=F32)
    ar = are_ref[0]
    ai = aim_ref[0]

    def body(i, carry):
        xr, xi = carry
        base = pl.multiple_of(i * 8, 8)
        blk = xin_ref[pl.ds(base, 8), :]
        rows_r, rows_i = [], []
        for rr in range(8):
            rows_r.append(xr)
            rows_i.append(xi)
            xr, xi = (ar * xr - ai * xi + blk[rr:rr + 1, :half],
                      ar * xi + ai * xr + blk[rr:rr + 1, half:])
        xs_ref[pl.ds(base, 8), :] = jnp.concatenate(
            [jnp.concatenate(rows_r, axis=0), jnp.concatenate(rows_i, axis=0)], axis=1)
        return xr, xi

    xr, xi = lax.fori_loop(0, n_chunks // 8, body, (st_ref[:, :half], st_ref[:, half:]))
    st_ref[...] = jnp.concatenate([xr, xi], axis=1)
    y2 = (jnp.dot(u2, t_ref[0], preferred_element_type=F32)
          + jnp.dot(xs_ref[...].astype(BF16), c_ref[0], preferred_element_type=F32))
    dd = d_ref[0]
    for l in range(chunk):
        y_ref[0, pl.ds(l, n_chunks, stride=chunk), :] = y2[:, l * LANES:(l + 1) * LANES] + dd * pieces[l]

    @pl.when(r == pl.num_programs(2) - 1)
    def _():
        hfin_ref[0, 0] = st_ref[...]


def _ssm_call(u, toep, to_state, from_state, dec_re, dec_im, d_skip, rows_per_step):
    b, s, d_ssm = u.shape
    nt = d_ssm // LANES
    chunk = toep.shape[1] // LANES
    n_chunks = rows_per_step // chunk
    nstate = to_state.shape[2]
    tile = lambda *shape: pl.BlockSpec((1,) + shape, lambda j, bi, r: (j, 0, 0))
    useq = pl.BlockSpec((1, rows_per_step, LANES), lambda j, bi, r: (bi, r, j))
    return pl.pallas_call(
        functools.partial(_ssm_kernel, chunk=chunk, n_chunks=n_chunks),
        grid=(nt, b, s // rows_per_step),
        in_specs=[useq, tile(chunk * LANES, chunk * LANES), tile(chunk * LANES, nstate),
                  tile(nstate, chunk * LANES), tile(1, nstate // 2), tile(1, nstate // 2), tile(1, LANES)],
        out_specs=[useq, pl.BlockSpec((1, 1, 1, nstate), lambda j, bi, r: (j, bi, 0, 0))],
        out_shape=[jax.ShapeDtypeStruct((b, s, d_ssm), F32), jax.ShapeDtypeStruct((nt, b, 1, nstate), F32)],
        scratch_shapes=[pltpu.VMEM((n_chunks, nstate), F32), pltpu.VMEM((n_chunks, nstate), F32),
                        pltpu.VMEM((1, nstate), F32)],
        compiler_params=_params("arbitrary", "arbitrary", "arbitrary"),
        name="ssm",
    )(u, toep, to_state, from_state, dec_re, dec_im, d_skip)


def _ssm_step_kernel(u_ref, hre_ref, him_ref, bre_ref, bim_ref, lre_ref, lim_ref, cre_ref, cim_ref, d_ref,
                     y_ref, xre_ref, xim_ref):
    u = u_ref[...]
    hr, hi = hre_ref[...], him_ref[...]
    lr, li = lre_ref[...], lim_ref[...]
    xr = lr * hr - li * hi + jnp.dot(u, bre_ref[...], precision=HI, preferred_element_type=F32)
    xi = lr * hi + li * hr + jnp.dot(u, bim_ref[...], precision=HI, preferred_element_type=F32)
    xre_ref[...] = xr
    xim_ref[...] = xi
    y_ref[...] = (jnp.dot(xr, cre_ref[...], precision=HI, preferred_element_type=F32)
                  - jnp.dot(xi, cim_ref[...], precision=HI, preferred_element_type=F32) + d_ref[...] * u)


def _ssm_step_call(u, h_re, h_im, lam_bar, b_bar, c, d_skip):
    rows, d_ssm = u.shape
    g, p = lam_bar[0].shape
    eye = jnp.eye(g, dtype=F32)
    bmat = [jnp.einsum("gpk,gf->gkfp", part, eye, precision=HI).reshape(d_ssm, g * p) for part in b_bar]
    cmat = [jnp.einsum("ghp,gf->gpfh", part, eye, precision=HI).reshape(g * p, d_ssm) for part in c]
    return pl.pallas_call(
        _ssm_step_kernel,
        out_shape=[jax.ShapeDtypeStruct((rows, d_ssm), F32), jax.ShapeDtypeStruct((rows, g * p), F32),
                   jax.ShapeDtypeStruct((rows, g * p), F32)],
        compiler_params=pltpu.CompilerParams(vmem_limit_bytes=VMEM_LIMIT),
        name="ssm_step",
    )(u, h_re.reshape(rows, g * p), h_im.reshape(rows, g * p), bmat[0], bmat[1],
      lam_bar[0].reshape(1, g * p), lam_bar[1].reshape(1, g * p), cmat[0], cmat[1], d_skip.reshape(1, d_ssm))


DEC_PAGES = 16


def _dec_kernel(pt_ref, q_ref, kn_ref, qt_ref, vnt_ref, lfn_ref, sli_ref, *rest, n_heads, pages):
    del pt_ref
    kt_refs, vt_refs, lf_refs = rest[:pages], rest[pages:2 * pages], rest[2 * pages:3 * pages]
    o_ref, qb_ref, m_ref, l_ref, acc_ref, carry_ref = rest[3 * pages:]
    step = pl.program_id(1)
    scale = HEAD_DIM ** -0.5

    @pl.when(step == 0)
    def _():
        qt = qt_ref[0] * scale
        vnt = vnt_ref[0]
        lane = lax.broadcasted_iota(jnp.int32, (HEAD_DIM, LANES), 1)
        for h in range(n_heads):
            qb_ref[h] = jnp.broadcast_to(qt[:, h:h + 1], (HEAD_DIM, LANES))
            acc_ref[h] = jnp.where(lane == 0, jnp.broadcast_to(vnt[:, h:h + 1], (HEAD_DIM, LANES)), 0.0)
        m_ref[...] = jnp.sum(q_ref[0] * kn_ref[0], axis=1, keepdims=True) * scale
        l_ref[...] = jnp.ones_like(l_ref)
        carry_ref[...] = lfn_ref[0]

    lf_all = jnp.concatenate([r[0, 0] for r in lf_refs], axis=0)
    sli = sli_ref[...]
    incl = sum(jnp.dot(piece, sli, preferred_element_type=F32) for piece in _split3(lf_all))
    run = carry_ref[...]
    scores = [None] * pages
    for i in reversed(range(pages)):
        rows = slice(i * n_heads, (i + 1) * n_heads)
        bias = incl[rows] - lf_all[rows] + run
        run = run + incl[rows][:, 0:1]
        qk = [jnp.sum(kt_refs[i][0, 0, h] * qb_ref[h], axis=0, keepdims=True) for h in range(n_heads)]
        scores[i] = jnp.concatenate(qk, axis=0) + bias
    carry_ref[...] = run

    m_old = m_ref[...]
    m_new = jnp.maximum(m_old, jnp.max(functools.reduce(jnp.maximum, scores), axis=1, keepdims=True))
    corr = jnp.exp(m_old - m_new)
    probs = [jnp.exp(s - m_new) for s in scores]
    l_ref[...] = l_ref[...] * corr + jnp.sum(functools.reduce(jnp.add, probs), axis=1, keepdims=True)
    m_ref[...] = m_new
    for h in range(n_heads):
        a = acc_ref[h] * corr[h:h + 1, :]
        for i in range(pages):
            a = a + vt_refs[i][0, 0, h] * probs[i][h:h + 1, :]
        acc_ref[h] = a

    @pl.when(step == pl.num_programs(1) - 1)
    def _():
        inv = 1.0 / l_ref[...]
        for h in range(n_heads):
            o_ref[0, h] = jnp.sum(acc_ref[h], axis=1, keepdims=True) * inv[h:h + 1, :]


def _dec_call(page_table, q, kn, qt, vnt, lfn, cache_k, cache_v, cache_logf, pages):
    db, n_pages = page_table.shape
    _, _, page, n_heads, _ = cache_k.shape
    assert page == LANES and n_pages % pages == 0
    kt = jnp.transpose(cache_k, (0, 1, 3, 4, 2))
    vt = jnp.transpose(cache_v, (0, 1, 3, 4, 2))
    lft = jnp.transpose(cache_logf, (0, 1, 3, 2))
    sli = jnp.asarray(np.tril(np.ones((page, page), np.float32)), BF16)
    pt_flat = page_table.reshape(-1)

    def phys(i):
        return lambda bi, st, pt: pt[bi * n_pages + n_pages - pages * (st + 1) + i]

    per_b = lambda *shape: pl.BlockSpec((1,) + shape, lambda bi, st, pt: (bi,) + (0,) * len(shape))
    kv_spec = lambda i: pl.BlockSpec((1, 1, n_heads, HEAD_DIM, page),
                                     lambda bi, st, pt, f=phys(i): (0, f(bi, st, pt), 0, 0, 0))
    lf_spec = lambda i: pl.BlockSpec((1, 1, n_heads, page), lambda bi, st, pt, f=phys(i): (0, f(bi, st, pt), 0, 0))
    grid_spec = pltpu.PrefetchScalarGridSpec(
        num_scalar_prefetch=1,
        grid=(db, n_pages // pages),
        in_specs=([per_b(n_heads, HEAD_DIM), per_b(n_heads, HEAD_DIM), per_b(HEAD_DIM, n_heads),
                   per_b(HEAD_DIM, n_heads), per_b(n_heads, 1), pl.BlockSpec((page, page), lambda bi, st, pt: (0, 0))]
                  + [kv_spec(i) for i in range(pages)] + [kv_spec(i) for i in range(pages)]
                  + [lf_spec(i) for i in range(pages)]),
        out_specs=per_b(n_heads, HEAD_DIM, 1),
        scratch_shapes=[pltpu.VMEM((n_heads, HEAD_DIM, page), F32), pltpu.VMEM((n_heads, 1), F32),
                        pltpu.VMEM((n_heads, 1), F32), pltpu.VMEM((n_heads, HEAD_DIM, page), F32),
                        pltpu.VMEM((n_heads, 1), F32)])
    return pl.pallas_call(
        functools.partial(_dec_kernel, n_heads=n_heads, pages=pages),
        grid_spec=grid_spec,
        out_shape=jax.ShapeDtypeStruct((db, n_heads, HEAD_DIM, 1), F32),
        compiler_params=_params("arbitrary", "arbitrary"),
        name="dec",
    )(pt_flat, q, kn, qt, vnt, lfn, sli, *([kt] * pages), *([vt] * pages), *([lft] * pages))


def _post_kernel(x_ref, o_ref, ys_ref, mod_ref, wglu_ref, bglu_ref, gatt_ref, gssm_ref, wout_ref, gpm_ref,
                 gpf_ref, wgu_ref, wdn_ref, gpo_ref, y_ref, *, n_heads, ff_chunk):
    x = x_ref[0]
    pairs = []
    for j in range(n_heads // 2):
        even = o_ref[0, 2 * j].astype(F32)
        odd = o_ref[0, 2 * j + 1].astype(F32)
        pairs.append(even + pltpu.roll(odd, HEAD_DIM, 1))
    attn = jnp.concatenate(pairs, axis=1)
    gl = _gelu_tanh(ys_ref[0])
    gl = gl * _sigmoid(jnp.dot(gl.astype(BF16), wglu_ref[...], preferred_element_type=F32) + bglu_ref[...])
    mix = jnp.concatenate([_rms(attn) * gatt_ref[...], _rms(gl) * gssm_ref[...]], axis=1)
    mo = jnp.dot(mix.astype(BF16), wout_ref[...], preferred_element_type=F32)
    x1 = x + mod_ref[0, 2] * (_rms(mo) * gpm_ref[...])
    h2 = (_rms(x1) * gpf_ref[...] * (1.0 + mod_ref[0, 4]) + mod_ref[0, 3]).astype(BF16)
    d_ff = wdn_ref.shape[0]
    acc = jnp.zeros_like(x)
    for c0 in range(0, d_ff, ff_chunk):
        gate = jnp.dot(h2, wgu_ref[:, c0:c0 + ff_chunk], preferred_element_type=F32)
        up = jnp.dot(h2, wgu_ref[:, d_ff + c0:d_ff + c0 + ff_chunk], preferred_element_type=F32)
        acc = acc + jnp.dot((_silu(gate) * up).astype(BF16), wdn_ref[c0:c0 + ff_chunk, :],
                            preferred_element_type=F32)
    y_ref[0] = x1 + mod_ref[0, 5] * (_rms(acc) * gpo_ref[...])


def _post_call(x, o, ys, mod, w_glu, b_glu, g_attn, g_ssm, w_out, g_post_mix, g_pre_ffn, w_gate_up, w_down,
               g_post_ffn, tm, ff_chunk):
    nb, s, d = x.shape
    n_heads = o.shape[1]
    d_ssm = ys.shape[2]
    mod_rows = mod.shape[2]
    const = lambda a: pl.BlockSpec(a.shape, lambda bi, ti: (0,) * a.ndim, pipeline_mode=pl.Buffered(1))
    rows = lambda width: pl.BlockSpec((1, tm, width), lambda bi, ti: (bi, ti, 0))
    mod_spec = (pl.BlockSpec((1, 6, 1, d), lambda bi, ti: (bi, 0, 0, 0)) if mod_rows == 1
                else pl.BlockSpec((1, 6, tm, d), lambda bi, ti: (bi, 0, ti, 0)))
    weights = (w_glu, b_glu, g_attn, g_ssm, w_out, g_post_mix, g_pre_ffn, w_gate_up, w_down, g_post_ffn)
    return pl.pallas_call(
        functools.partial(_post_kernel, n_heads=n_heads, ff_chunk=ff_chunk),
        grid=(nb, s // tm),
        in_specs=[rows(d), pl.BlockSpec((1, n_heads, tm, LANES), lambda bi, ti: (bi, 0, ti, 0)), rows(d_ssm),
                  mod_spec] + [const(w) for w in weights],
        out_specs=rows(d),
        out_shape=jax.ShapeDtypeStruct((nb, s, d), F32),
        compiler_params=_params("arbitrary", "arbitrary"),
        name="post",
    )(x, o, ys, mod, *weights)


def _row(v):
    return v.reshape(1, -1).astype(F32)


def _layer(xp, xs, cp, cs, cache_k, cache_v, cache_logf, h_re, h_im, page_table, w):
    b, s, d = xp.shape
    db = xs.shape[0]
    n_heads = w["b_f"].shape[0]
    d_attn = n_heads * HEAD_DIM
    d_ff = w["w_down"].shape[0]
    g, p = w["a_re"].shape
    d_ssm = g * SSM_GROUP

    w_in = w["w_in"]
    w_qkv = w_in[:, :3 * d_attn].astype(BF16)
    w_f = jnp.pad(w_in[:, 3 * d_attn:3 * d_attn + n_heads].astype(F32), ((0, 0), (0, LANES - n_heads)))
    b_f = jnp.pad(_row(w["b_f"]), ((0, 0), (0, LANES - n_heads)))
    w_u = w_in[:, 3 * d_attn + n_heads:].astype(BF16)
    post_w = (w["w_glu"].astype(BF16), _row(w["b_glu"]), _row(w["g_attn_out"]), _row(w["g_ssm_out"]),
              w["w_out"].astype(BF16), _row(w["g_post_mix"]), _row(w["g_pre_ffn"]), w["w_gate_up"].astype(BF16),
              w["w_down"].astype(BF16), _row(w["g_post_ffn"]))
    g_pre = _row(w["g_pre_mix"])

    n_cond = b + db
    pad_rows = -n_cond % 8
    c_all = jnp.pad(jnp.concatenate([cp, cs], axis=0).astype(F32), ((0, pad_rows), (0, 0)))
    mod = _mod_call(c_all, w["w_ada"].astype(F32), _row(w["b_ada"]))
    mod_p = mod[:b].reshape(b, 6, 1, d)
    mod_s = mod[b:n_cond].reshape(db, 6, d).transpose(1, 0, 2)[None]

    tm = min(512, s)
    k_p, v_p, lf_p, qa, ka, va, u_p = _pre_prompt_call(xp, mod_p, g_pre, w_qkv, w_f, b_f, w_u, n_heads, tm)
    o_p = _flash_call(qa, ka, va, min(512, s))
    lam_dt, lam_bar, b_bar, c = _ssm_discretize(w["a_re"], w["a_im"], w["log_dt"], w["b_re"], w["b_im"],
                                                w["c_re"], w["c_im"])
    toep, to_state, from_state, dec_re, dec_im = _ssm_chunk_operators(lam_dt, b_bar, c, SSM_CHUNK)
    nt = d_ssm // LANES
    ys_p, hfin = _ssm_call(u_p, toep, to_state, from_state, dec_re, dec_im,
                           w["d_skip"].astype(F32).reshape(nt, 1, LANES), min(2048, s))
    y_p = _post_call(xp, o_p, ys_p, mod_p, *post_w, tm=tm, ff_chunk=d_ff // 2)
    half = hfin.shape[-1] // 2
    hre_p = hfin[:, :, 0, :half].transpose(1, 0, 2).reshape(b, g, p)
    him_p = hfin[:, :, 0, half:].transpose(1, 0, 2).reshape(b, g, p)

    xs2 = xs.reshape(db, d)
    z_s, lf_s, u_s = _pre_sample_call(xs2, mod_s, g_pre, w_qkv, w_f, b_f, w_u)
    q_s = z_s[:, :d_attn].reshape(db, n_heads, HEAD_DIM)
    k_s = z_s[:, d_attn:2 * d_attn].reshape(db, n_heads, HEAD_DIM)
    v_s = z_s[:, 2 * d_attn:].reshape(db, n_heads, HEAD_DIM)
    lfn = lf_s[:, :n_heads]
    n_pages = page_table.shape[1]
    o_s = _dec_call(page_table, q_s, k_s, q_s.transpose(0, 2, 1), v_s.transpose(0, 2, 1),
                    lfn.reshape(db, n_heads, 1), cache_k, cache_v, cache_logf, math.gcd(DEC_PAGES, n_pages))
    o_s = o_s.reshape(db, n_heads, HEAD_DIM)
    o_s = jnp.pad(o_s.transpose(1, 0, 2), ((0, 0), (0, 0), (0, LANES - HEAD_DIM))).astype(BF16)[None]
    ys_s, hre_s, him_s = _ssm_step_call(u_s, h_re, h_im, lam_bar, b_bar, c, w["d_skip"].astype(F32))
    y_s = _post_call(xs2[None], o_s, ys_s[None], mod_s, *post_w, tm=db, ff_chunk=d_ff // 2)

    return dict(
        y_p=y_p, y_s=y_s.reshape(db, 1, d),
        k_p=k_p.transpose(0, 1, 4, 2, 3), v_p=v_p.transpose(0, 1, 4, 2, 3),
        f_p=lf_p.transpose(0, 1, 3, 2), r_p=hre_p, i_p=him_p,
        k_s=k_s.reshape(db, 1, n_heads, HEAD_DIM), v_s=v_s.reshape(db, 1, n_heads, HEAD_DIM),
        f_s=lfn.reshape(db, 1, n_heads), r_s=hre_s.reshape(db, g, p), i_s=him_s.reshape(db, g, p))


def kernel(x_prompt, x_sample, c_prompt, c_sample, cache_k, cache_v, cache_logf, state_ssm_re, state_ssm_im,
           page_table, w_ada, b_ada, g_pre_mix, g_post_mix, g_pre_ffn, g_post_ffn, w_in, b_f, a_re, a_im,
           log_dt, b_re, b_im, c_re, c_im, d_skip, w_glu, b_glu, g_attn_out, g_ssm_out, w_out, w_gate_up, w_down):
    depth = w_in.shape[0]
    assert depth == 1 and x_sample.shape[1] == 1, "single layer, one new token per sequence"
    weights = dict(w_ada=w_ada, b_ada=b_ada, g_pre_mix=g_pre_mix, g_post_mix=g_post_mix, g_pre_ffn=g_pre_ffn,
                   g_post_ffn=g_post_ffn, w_in=w_in, b_f=b_f, a_re=a_re, a_im=a_im, log_dt=log_dt, b_re=b_re,
                   b_im=b_im, c_re=c_re, c_im=c_im, d_skip=d_skip, w_glu=w_glu, b_glu=b_glu,
                   g_attn_out=g_attn_out, g_ssm_out=g_ssm_out, w_out=w_out, w_gate_up=w_gate_up, w_down=w_down)
    w0 = {name: val[0] for name, val in weights.items()}
    r = _layer(x_prompt.astype(F32), x_sample.astype(F32), c_prompt, c_sample, cache_k, cache_v, cache_logf,
               state_ssm_re[0], state_ssm_im[0], page_table, w0)
    stack = lambda a: a[None]
    return (r["y_p"].astype(x_prompt.dtype), r["y_s"].astype(x_sample.dtype),
            stack(r["k_p"]), stack(r["v_p"]), stack(r["f_p"]), stack(r["r_p"]), stack(r["i_p"]),
            stack(r["k_s"]), stack(r["v_s"]), stack(r["f_s"]), stack(r["r_s"]), stack(r["i_s"]))
```

```python
import functools
import math

import numpy as np
import jax
import jax.numpy as jnp
from jax import lax
from jax.experimental import pallas as pl
from jax.experimental.pallas import tpu as pltpu

F32 = jnp.float32
BF16 = jnp.bfloat16
HI = lax.Precision.HIGHEST
EPS = 1e-6
NEG = -1e30
HEAD_DIM = 64
SSM_GROUP = 16
LANES = 128
GROUPS_PER_TILE = LANES // SSM_GROUP
VMEM_LIMIT = 56 * 1024 * 1024
SSM_CHUNK = 8
V_ROWS = 80
SQRT_2_OVER_PI = math.sqrt(2.0 / math.pi)
LOG2_E = math.log2(math.e)


def _sigmoid(x):
    return 1.0 / (1.0 + jnp.exp(-x))


def _silu(x):
    return x * _sigmoid(x)


def _log_sigmoid(x):
    return jnp.minimum(x, 0.0) - jnp.log1p(jnp.exp(-jnp.abs(x)))


def _gelu_tanh(x):
    return x * (0.5 * (1.0 + jnp.tanh(SQRT_2_OVER_PI * (x + 0.044715 * (x * x * x)))))


def _rms(x):
    return x * lax.rsqrt(jnp.mean(x * x, axis=-1, keepdims=True) + EPS)


def _params(*sem):
    return pltpu.CompilerParams(dimension_semantics=sem, vmem_limit_bytes=VMEM_LIMIT)


def _mod_kernel(c_ref, w_ref, b_ref, o_ref):
    c = c_ref[...]
    o_ref[...] = jnp.dot(_silu(c), w_ref[...], precision=HI, preferred_element_type=F32) + b_ref[...]


def _mod_call(c_all, w_ada, b_ada):
    rows, d = c_all.shape
    n = w_ada.shape[1]
    return pl.pallas_call(
        _mod_kernel,
        grid=(n // d,),
        in_specs=[pl.BlockSpec((rows, d), lambda i: (0, 0)),
                  pl.BlockSpec((d, d), lambda i: (0, i)),
                  pl.BlockSpec((1, d), lambda i: (0, i))],
        out_specs=pl.BlockSpec((rows, d), lambda i: (0, i)),
        out_shape=jax.ShapeDtypeStruct((rows, n), F32),
        compiler_params=_params("arbitrary"),
        name="mod",
    )(c_all, w_ada, b_ada)


def _split3(f):
    hi = f.astype(BF16)
    r1 = f - hi.astype(F32)
    mid = r1.astype(BF16)
    lo = (r1 - mid.astype(F32)).astype(BF16)
    return hi, mid, lo


def _pre_prompt_kernel(x_ref, mod_ref, g_ref, wqkv_ref, wfh_ref, wfl_ref, bf_ref, wu_ref, tri_ref, pq_ref, pk_ref,
                       cq_ref, ck_ref, kt_ref, vt_ref, lft_ref, qa_ref, ka_ref, va_ref, u_ref, carry_ref,
                       *, n_heads, d_attn):
    @pl.when(pl.program_id(1) == 0)
    def _():
        carry_ref[...] = jnp.zeros_like(carry_ref)

    x = x_ref[0]
    tm = x.shape[0]
    h = _rms(x) * g_ref[...] * (1.0 + mod_ref[0, 1]) + mod_ref[0, 0]
    hb = h.astype(BF16)
    z = jnp.dot(hb, wqkv_ref[...], preferred_element_type=F32)
    u_ref[0] = jnp.dot(hb, wu_ref[...], preferred_element_type=F32)

    h_lo = (h - hb.astype(F32)).astype(BF16)
    fl = (jnp.dot(hb, wfh_ref[...], preferred_element_type=F32) + jnp.dot(hb, wfl_ref[...], preferred_element_type=F32)
          + jnp.dot(h_lo, wfh_ref[...], preferred_element_type=F32) + bf_ref[...])
    lane = lax.broadcasted_iota(jnp.int32, (tm, LANES), 1)
    logf = jnp.where(lane < n_heads, _log_sigmoid(fl), 0.0)
    logf_t = logf.T
    for pg in range(tm // LANES):
        lft_ref[0, pg] = logf_t[:n_heads, pg * LANES:(pg + 1) * LANES]
    tri_sum = jnp.dot(tri_ref[...], jnp.concatenate(_split3(logf), axis=1), preferred_element_type=F32)
    cum = tri_sum[:, :LANES] + tri_sum[:, LANES:2 * LANES] + tri_sum[:, 2 * LANES:] + carry_ref[...]
    carry_ref[...] = cum[tm - 1:tm, :]

    fs = jnp.concatenate(_split3(cum * LOG2_E), axis=1)
    augq = jnp.dot(fs, pq_ref[...], preferred_element_type=F32) + cq_ref[...]
    augk = jnp.dot(fs, pk_ref[...], preferred_element_type=F32) + ck_ref[...]
    low = lane < HEAD_DIM
    vone = jnp.where(lane == HEAD_DIM, 1.0, 0.0)
    scale = HEAD_DIM ** -0.5 * LOG2_E
    for j in range(n_heads // 2):
        zq = z[:, j * LANES:(j + 1) * LANES] * scale
        zk = z[:, d_attn + j * LANES:d_attn + (j + 1) * LANES]
        zv = z[:, 2 * d_attn + j * LANES:2 * d_attn + (j + 1) * LANES]
        for par in range(2):
            hh = 2 * j + par
            if par:
                zq, zk, zv = (pltpu.roll(a, HEAD_DIM, 1) for a in (zq, zk, zv))
            qa_ref[0, hh] = jnp.where(low, zq, augq[:, hh * LANES:(hh + 1) * LANES]).astype(BF16)
            ka_ref[0, hh] = jnp.where(low, zk, augk[:, hh * LANES:(hh + 1) * LANES]).astype(BF16)
            k_t = zk.T
            v_t = jnp.where(low, zv, vone).T
            va_ref[0, hh, 0] = v_t[:V_ROWS].astype(BF16)
            for pg in range(tm // LANES):
                kt_ref[0, pg, hh] = k_t[:HEAD_DIM, pg * LANES:(pg + 1) * LANES]
                vt_ref[0, pg, hh] = v_t[:HEAD_DIM, pg * LANES:(pg + 1) * LANES]


def _aug_constants(n_heads):
    pq = np.zeros((3 * LANES, n_heads * LANES), np.float32)
    pk = np.zeros((3 * LANES, n_heads * LANES), np.float32)
    cq = np.zeros((1, n_heads * LANES), np.float32)
    ck = np.zeros((1, n_heads * LANES), np.float32)
    for h in range(n_heads):
        for piece in range(3):
            pq[piece * LANES + h, h * LANES + HEAD_DIM + piece] = 1.0
            pk[piece * LANES + h, h * LANES + HEAD_DIM + 3 + piece] = -1.0
            cq[0, h * LANES + HEAD_DIM + 3 + piece] = 1.0
            ck[0, h * LANES + HEAD_DIM + piece] = 1.0
    return jnp.asarray(pq, BF16), jnp.asarray(pk, BF16), jnp.asarray(cq), jnp.asarray(ck)


def _pre_prompt_call(x, mod_p, g_pre, w_qkv, w_f, b_f, w_u, n_heads, tm):
    b, s, d = x.shape
    d_attn = n_heads * HEAD_DIM
    d_ssm = w_u.shape[1]
    tri = jnp.asarray(np.tril(np.ones((tm, tm), np.float32)), BF16)
    w_f_hi = w_f.astype(BF16)
    w_f_lo = (w_f - w_f_hi.astype(F32)).astype(BF16)
    pq, pk, cq, ck = _aug_constants(n_heads)
    pages = tm // LANES
    const = lambda *shape: pl.BlockSpec(shape, lambda bi, ti: (0,) * len(shape))
    rows = lambda width: pl.BlockSpec((1, tm, width), lambda bi, ti: (bi, ti, 0))
    heads = pl.BlockSpec((1, n_heads, tm, LANES), lambda bi, ti: (bi, 0, ti, 0))
    heads_t = pl.BlockSpec((1, n_heads, 1, V_ROWS, tm), lambda bi, ti: (bi, 0, ti, 0, 0))
    paged = pl.BlockSpec((1, pages, n_heads, HEAD_DIM, LANES), lambda bi, ti: (bi, ti, 0, 0, 0))
    paged_shape = jax.ShapeDtypeStruct((b, s // LANES, n_heads, HEAD_DIM, LANES), F32)
    aug_shape = jax.ShapeDtypeStruct((b, n_heads, s, LANES), BF16)
    aug_t_shape = jax.ShapeDtypeStruct((b, n_heads, s // tm, V_ROWS, tm), BF16)
    return pl.pallas_call(
        functools.partial(_pre_prompt_kernel, n_heads=n_heads, d_attn=d_attn),
        grid=(b, s // tm),
        in_specs=[rows(d),
                  pl.BlockSpec((1, 6, 1, d), lambda bi, ti: (bi, 0, 0, 0)),
                  const(1, d), const(d, 3 * d_attn), const(d, LANES), const(d, LANES), const(1, LANES),
                  const(d, d_ssm), const(tm, tm), const(3 * LANES, n_heads * LANES),
                  const(3 * LANES, n_heads * LANES), const(1, n_heads * LANES), const(1, n_heads * LANES)],
        out_specs=[paged, paged, pl.BlockSpec((1, pages, n_heads, LANES), lambda bi, ti: (bi, ti, 0, 0)),
                   heads, heads, heads_t, rows(d_ssm)],
        out_shape=[paged_shape, paged_shape, jax.ShapeDtypeStruct((b, s // LANES, n_heads, LANES), F32),
                   aug_shape, aug_shape, aug_t_shape, jax.ShapeDtypeStruct((b, s, d_ssm), F32)],
        scratch_shapes=[pltpu.VMEM((1, LANES), F32)],
        compiler_params=_params("arbitrary", "arbitrary"),
        name="pre_prompt",
    )(x, mod_p, g_pre, w_qkv, w_f_hi, w_f_lo, b_f, w_u, tri, pq, pk, cq, ck)


def _pre_sample_kernel(x_ref, mod_ref, g_ref, wqkv_ref, wf_ref, bf_ref, wu_ref, z_ref, lf_ref, u_ref):
    x = x_ref[...]
    h = _rms(x) * g_ref[...] * (1.0 + mod_ref[0, 1]) + mod_ref[0, 0]
    hb = h.astype(BF16)
    z_ref[...] = jnp.dot(hb, wqkv_ref[...], preferred_element_type=F32)
    u_ref[...] = jnp.dot(hb, wu_ref[...], preferred_element_type=F32)
    fl = jnp.dot(h, wf_ref[...], precision=HI, preferred_element_type=F32) + bf_ref[...]
    lf_ref[...] = _log_sigmoid(fl)


def _pre_sample_call(x, mod_s, g_pre, w_qkv, w_f, b_f, w_u):
    rows = x.shape[0]
    return pl.pallas_call(
        _pre_sample_kernel,
        out_shape=[jax.ShapeDtypeStruct((rows, w_qkv.shape[1]), F32),
                   jax.ShapeDtypeStruct((rows, LANES), F32),
                   jax.ShapeDtypeStruct((rows, w_u.shape[1]), F32)],
        compiler_params=pltpu.CompilerParams(vmem_limit_bytes=VMEM_LIMIT),
        name="pre_sample",
    )(x, mod_s, g_pre, w_qkv, w_f, b_f, w_u)


FLASH_HEADS = 4


def _flash_kernel(q_ref, k_ref, vt_ref, o_ref, st_ref, m_ref, acc_ref, *, tq):
    qi = pl.program_id(2)
    heads = q_ref.shape[1]

    def put_scores(kb, buf):
        off = pl.multiple_of(kb * tq, tq)
        for h in range(heads):
            st_ref[h, buf] = lax.dot_general(k_ref[0, h, pl.ds(off, tq), :], q_ref[0, h], (((1,), (1,)), ((), ())),
                                             preferred_element_type=F32)

    def update(kb, buf, diagonal):
        for h in range(heads):
            st = st_ref[h, buf]
            if diagonal:
                key = lax.broadcasted_iota(jnp.int32, st.shape, 0)
                qry = lax.broadcasted_iota(jnp.int32, st.shape, 1)
                st = jnp.where(key <= qry, st, NEG)
            m = m_ref[h]
            m_new = jnp.maximum(m, jnp.max(st, axis=0, keepdims=True))
            acc_ref[h] = acc_ref[h] * jnp.exp2(m - m_new) + jnp.dot(
                vt_ref[0, h, kb], jnp.exp2(st - m_new).astype(BF16), preferred_element_type=F32)
            m_ref[h] = m_new

    def finish():
        for h in range(heads):
            acc = acc_ref[h]
            o = acc[:HEAD_DIM] * (1.0 / acc[HEAD_DIM:HEAD_DIM + 1, :])
            o_ref[0, h] = jnp.concatenate([o, jnp.zeros((LANES - HEAD_DIM, tq), F32)], axis=0).T.astype(BF16)

    m_ref[...] = jnp.full_like(m_ref, NEG)
    acc_ref[...] = jnp.zeros_like(acc_ref)
    put_scores(0, 0)

    def pair(j, carry):
        put_scores(2 * j + 1, 1)
        update(2 * j, 0, False)
        put_scores(2 * j + 2, 0)
        update(2 * j + 1, 1, False)
        return carry

    lax.fori_loop(0, qi // 2, pair, 0)

    @pl.when(qi % 2 == 1)
    def _():
        put_scores(qi, 1)
        update(qi - 1, 0, False)
        update(qi, 1, True)
        finish()

    @pl.when(qi % 2 == 0)
    def _():
        update(qi, 0, True)
        finish()


def _flash_call(qa, ka, vat, tq):
    b, h, s, _ = qa.shape
    assert vat.shape[-1] == tq, "value chunks are laid out per key block"
    nh = math.gcd(FLASH_HEADS, h)
    qspec = pl.BlockSpec((1, nh, tq, LANES), lambda bi, hi, qi: (bi, hi, qi, 0))
    once = pl.Buffered(1)
    kspec = pl.BlockSpec((1, nh, s, LANES), lambda bi, hi, qi: (bi, hi, 0, 0), pipeline_mode=once)
    vspec = pl.BlockSpec((1, nh, s // tq, V_ROWS, tq), lambda bi, hi, qi: (bi, hi, 0, 0, 0), pipeline_mode=once)
    return pl.pallas_call(
        functools.partial(_flash_kernel, tq=tq),
        grid=(b, h // nh, s // tq),
        in_specs=[qspec, kspec, vspec],
        out_specs=qspec,
        out_shape=jax.ShapeDtypeStruct((b, h, s, LANES), BF16),
        scratch_shapes=[pltpu.VMEM((nh, 2, tq, tq), F32), pltpu.VMEM((nh, 1, tq), F32),
                        pltpu.VMEM((nh, V_ROWS, tq), F32)],
        compiler_params=_params("arbitrary", "arbitrary", "arbitrary"),
        name="flash",
    )(qa, ka, vat)


def _cmul(ar, ai, br, bi):
    return ar * br - ai * bi, ar * bi + ai * br


def _ssm_discretize(a_re, a_im, log_dt, b_re, b_im, c_re, c_im):
    ar, ai = a_re.astype(F32), a_im.astype(F32)
    dt = jnp.exp(log_dt.astype(F32))[:, None]
    lam_dt = (ar * dt, ai * dt)
    mag = jnp.exp(lam_dt[0])
    lam_bar = (mag * jnp.cos(lam_dt[1]), mag * jnp.sin(lam_dt[1]))
    den = ar * ar + ai * ai
    nr, ni = lam_bar[0] - 1.0, lam_bar[1]
    coef = ((nr * ar + ni * ai) / den, (ni * ar - nr * ai) / den)
    b_bar = _cmul(coef[0][..., None], coef[1][..., None], b_re.astype(F32), b_im.astype(F32))
    c = (c_re.astype(F32), c_im.astype(F32))
    return lam_dt, lam_bar, b_bar, c


def _ssm_chunk_operators(lam_dt, b_bar, c, chunk):
    g, p = lam_dt[0].shape
    nt = g // GROUPS_PER_TILE
    steps = jnp.arange(chunk + 1, dtype=F32)[:, None, None]
    mag = jnp.exp(steps * lam_dt[0][None])
    pw = (mag * jnp.cos(steps * lam_dt[1][None]), mag * jnp.sin(steps * lam_dt[1][None]))
    eye = jnp.eye(GROUPS_PER_TILE, dtype=F32)
    pb = _cmul(pw[0][:chunk, :, :, None], pw[1][:chunk, :, :, None], b_bar[0][None], b_bar[1][None])
    kmat = (jnp.einsum("ghp,dgpk->gdhk", c[0], pb[0], precision=HI)
            - jnp.einsum("ghp,dgpk->gdhk", c[1], pb[1], precision=HI))
    kd = kmat.reshape(nt, GROUPS_PER_TILE, chunk, SSM_GROUP, SSM_GROUP)
    dblk = jnp.einsum("jgdhk,gf->jdgkfh", kd, eye, precision=HI).reshape(nt, chunk, LANES, LANES)
    lag = np.arange(chunk)[None, :] - np.arange(chunk)[:, None]
    tfull = dblk[:, np.clip(lag, 0, None)] * jnp.asarray(lag >= 0, F32)[None, :, :, None, None]
    toep = tfull.transpose(0, 1, 3, 2, 4).reshape(nt, chunk * LANES, chunk * LANES)
    rev = chunk - 1 - np.arange(chunk)
    wb = _cmul(pw[0][rev][..., None], pw[1][rev][..., None], b_bar[0][None], b_bar[1][None])
    to_state = jnp.concatenate(
        [jnp.einsum("ljgpk,gf->jlgkfp", part.reshape(chunk, nt, GROUPS_PER_TILE, p, SSM_GROUP), eye,
                    precision=HI).reshape(nt, chunk * LANES, GROUPS_PER_TILE * p) for part in wb], axis=-1)
    cp = _cmul(c[0][None], c[1][None], pw[0][1:chunk + 1][:, :, None, :], pw[1][1:chunk + 1][:, :, None, :])
    from_state = jnp.concatenate(
        [jnp.einsum("ljghp,gf->jgplfh", part.reshape(chunk, nt, GROUPS_PER_TILE, SSM_GROUP, p), eye,
                    precision=HI).reshape(nt, GROUPS_PER_TILE * p, chunk * LANES)
         for part in (cp[0], -cp[1])], axis=1)
    dec_re = pw[0][chunk].reshape(nt, 1, GROUPS_PER_TILE * p)
    dec_im = pw[1][chunk].reshape(nt, 1, GROUPS_PER_TILE * p)
    return toep.astype(BF16), to_state.astype(BF16), from_state.astype(BF16), dec_re, dec_im


def _ssm_kernel(u_ref, t_ref, g_ref, c_ref, are_ref, aim_ref, d_ref, y_ref, hfin_ref, xin_ref, xs_ref, st_ref,
                *, chunk, n_chunks):
    r = pl.program_id(2)

    @pl.when(r == 0)
    def _():
        st_ref[...] = jnp.zeros_like(st_ref)

    half = st_ref.shape[1] // 2
    pieces = [u_ref[0, pl.ds(l, n_chunks, stride=chunk), :] for l in range(chunk)]
    u2 = jnp.concatenate([pc.astype(BF16) for pc in pieces], axis=1)
    xin_ref[...] = jnp.dot(u2, g_ref[0], preferred_element_type=F32)
    ar = are_ref[0]
    ai = aim_ref[0]

    def body(i, carry):
        xr, xi = carry
        base = pl.multiple_of(i * 8, 8)
        blk = xin_ref[pl.ds(base, 8), :]
        rows_r, rows_i = [], []
        for rr in range(8):
            rows_r.append(xr)
            rows_i.append(xi)
            xr, xi = (ar * xr - ai * xi + blk[rr:rr + 1, :half],
                      ar * xi + ai * xr + blk[rr:rr + 1, half:])
        xs_ref[pl.ds(base, 8), :] = jnp.concatenate(
            [jnp.concatenate(rows_r, axis=0), jnp.concatenate(rows_i, axis=0)], axis=1)
        return xr, xi

    xr, xi = lax.fori_loop(0, n_chunks // 8, body, (st_ref[:, :half], st_ref[:, half:]), unroll=True)
    st_ref[...] = jnp.concatenate([xr, xi], axis=1)
    y2 = (jnp.dot(u2, t_ref[0], preferred_element_type=F32)
          + jnp.dot(xs_ref[...].astype(BF16), c_ref[0], preferred_element_type=F32))
    dd = d_ref[0]
    for l in range(chunk):
        y_ref[0, pl.ds(l, n_chunks, stride=chunk), :] = y2[:, l * LANES:(l + 1) * LANES] + dd * pieces[l]

    @pl.when(r == pl.num_programs(2) - 1)
    def _():
        hfin_ref[0, 0] = st_ref[...]


def _ssm_call(u, toep, to_state, from_state, dec_re, dec_im, d_skip, rows_per_step):
    b, s, d_ssm = u.shape
    nt = d_ssm // LANES
    chunk = toep.shape[1] // LANES
    n_chunks = rows_per_step // chunk
    nstate = to_state.shape[2]
    tile = lambda *shape: pl.BlockSpec((1,) + shape, lambda j, bi, r: (j, 0, 0))
    useq = pl.BlockSpec((1, rows_per_step, LANES), lambda j, bi, r: (bi, r, j))
    return pl.pallas_call(
        functools.partial(_ssm_kernel, chunk=chunk, n_chunks=n_chunks),
        grid=(nt, b, s // rows_per_step),
        in_specs=[useq, tile(chunk * LANES, chunk * LANES), tile(chunk * LANES, nstate),
                  tile(nstate, chunk * LANES), tile(1, nstate // 2), tile(1, nstate // 2), tile(1, LANES)],
        out_specs=[useq, pl.BlockSpec((1, 1, 1, nstate), lambda j, bi, r: (j, bi, 0, 0))],
        out_shape=[jax.ShapeDtypeStruct((b, s, d_ssm), F32), jax.ShapeDtypeStruct((nt, b, 1, nstate), F32)],
        scratch_shapes=[pltpu.VMEM((n_chunks, nstate), F32), pltpu.VMEM((n_chunks, nstate), F32),
                        pltpu.VMEM((1, nstate), F32)],
        compiler_params=_params("arbitrary", "arbitrary", "arbitrary"),
        name="ssm",
    )(u, toep, to_state, from_state, dec_re, dec_im, d_skip)


def _ssm_step_kernel(u_ref, hre_ref, him_ref, bre_ref, bim_ref, lre_ref, lim_ref, cre_ref, cim_ref, d_ref,
                     y_ref, xre_ref, xim_ref):
    u = u_ref[...]
    hr, hi = hre_ref[...], him_ref[...]
    lr, li = lre_ref[...], lim_ref[...]
    xr = lr * hr - li * hi + jnp.dot(u, bre_ref[...], precision=HI, preferred_element_type=F32)
    xi = lr * hi + li * hr + jnp.dot(u, bim_ref[...], precision=HI, preferred_element_type=F32)
    xre_ref[...] = xr
    xim_ref[...] = xi
    y_ref[...] = (jnp.dot(xr, cre_ref[...], precision=HI, preferred_element_type=F32)
                  - jnp.dot(xi, cim_ref[...], precision=HI, preferred_element_type=F32) + d_ref[...] * u)


def _ssm_step_call(u, h_re, h_im, lam_bar, b_bar, c, d_skip):
    rows, d_ssm = u.shape
    g, p = lam_bar[0].shape
    eye = jnp.eye(g, dtype=F32)
    bmat = [jnp.einsum("gpk,gf->gkfp", part, eye, precision=HI).reshape(d_ssm, g * p) for part in b_bar]
    cmat = [jnp.einsum("ghp,gf->gpfh", part, eye, precision=HI).reshape(g * p, d_ssm) for part in c]
    return pl.pallas_call(
        _ssm_step_kernel,
        out_shape=[jax.ShapeDtypeStruct((rows, d_ssm), F32), jax.ShapeDtypeStruct((rows, g * p), F32),
                   jax.ShapeDtypeStruct((rows, g * p), F32)],
        compiler_params=pltpu.CompilerParams(vmem_limit_bytes=VMEM_LIMIT),
        name="ssm_step",
    )(u, h_re.reshape(rows, g * p), h_im.reshape(rows, g * p), bmat[0], bmat[1],
      lam_bar[0].reshape(1, g * p), lam_bar[1].reshape(1, g * p), cmat[0], cmat[1], d_skip.reshape(1, d_ssm))


DEC_PAGES = 16


def _dec_kernel(pt_ref, q_ref, kn_ref, qt_ref, vnt_ref, lfn_ref, sli_ref, *rest, n_heads, pages):
    del pt_ref
    kt_refs, vt_refs, lf_refs = rest[:pages], rest[pages:2 * pages], rest[2 * pages:3 * pages]
    o_ref, qb_ref, m_ref, l_ref, acc_ref, carry_ref = rest[3 * pages:]
    step = pl.program_id(1)
    scale = HEAD_DIM ** -0.5

    @pl.when(step == 0)
    def _():
        qt = qt_ref[0] * scale
        vnt = vnt_ref[0]
        lane = lax.broadcasted_iota(jnp.int32, (HEAD_DIM, LANES), 1)
        for h in range(n_heads):
            qb_ref[h] = jnp.broadcast_to(qt[:, h:h + 1], (HEAD_DIM, LANES))
            acc_ref[h] = jnp.where(lane == 0, jnp.broadcast_to(vnt[:, h:h + 1], (HEAD_DIM, LANES)), 0.0)
        m_ref[...] = jnp.sum(q_ref[0] * kn_ref[0], axis=1, keepdims=True) * scale
        l_ref[...] = jnp.ones_like(l_ref)
        carry_ref[...] = lfn_ref[0]

    lf_all = jnp.concatenate([r[0, 0] for r in lf_refs], axis=0)
    sli = sli_ref[...]
    incl = sum(jnp.dot(piece, sli, preferred_element_type=F32) for piece in _split3(lf_all))
    run = carry_ref[...]
    scores = [None] * pages
    for i in reversed(range(pages)):
        rows = slice(i * n_heads, (i + 1) * n_heads)
        bias = incl[rows] - lf_all[rows] + run
        run = run + incl[rows][:, 0:1]
        qk = [jnp.sum(kt_refs[i][0, 0, h] * qb_ref[h], axis=0, keepdims=True) for h in range(n_heads)]
        scores[i] = jnp.concatenate(qk, axis=0) + bias
    carry_ref[...] = run

    m_old = m_ref[...]
    m_new = jnp.maximum(m_old, jnp.max(functools.reduce(jnp.maximum, scores), axis=1, keepdims=True))
    corr = jnp.exp(m_old - m_new)
    probs = [jnp.exp(s - m_new) for s in scores]
    l_ref[...] = l_ref[...] * corr + jnp.sum(functools.reduce(jnp.add, probs), axis=1, keepdims=True)
    m_ref[...] = m_new
    for h in range(n_heads):
        a = acc_ref[h] * corr[h:h + 1, :]
        for i in range(pages):
            a = a + vt_refs[i][0, 0, h] * probs[i][h:h + 1, :]
        acc_ref[h] = a

    @pl.when(step == pl.num_programs(1) - 1)
    def _():
        inv = 1.0 / l_ref[...]
        for h in range(n_heads):
            o_ref[0, h] = jnp.sum(acc_ref[h], axis=1, keepdims=True) * inv[h:h + 1, :]


def _dec_call(page_table, q, kn, qt, vnt, lfn, cache_k, cache_v, cache_logf, pages):
    db, n_pages = page_table.shape
    _, _, page, n_heads, _ = cache_k.shape
    assert page == LANES and n_pages % pages == 0
    kt = jnp.transpose(cache_k, (0, 1, 3, 4, 2))
    vt = jnp.transpose(cache_v, (0, 1, 3, 4, 2))
    lft = jnp.transpose(cache_logf, (0, 1, 3, 2))
    sli = jnp.asarray(np.tril(np.ones((page, page), np.float32)), BF16)
    pt_flat = page_table.reshape(-1)

    def phys(i):
        return lambda bi, st, pt: pt[bi * n_pages + n_pages - pages * (st + 1) + i]

    per_b = lambda *shape: pl.BlockSpec((1,) + shape, lambda bi, st, pt: (bi,) + (0,) * len(shape))
    kv_spec = lambda i: pl.BlockSpec((1, 1, n_heads, HEAD_DIM, page),
                                     lambda bi, st, pt, f=phys(i): (0, f(bi, st, pt), 0, 0, 0))
    lf_spec = lambda i: pl.BlockSpec((1, 1, n_heads, page), lambda bi, st, pt, f=phys(i): (0, f(bi, st, pt), 0, 0))
    grid_spec = pltpu.PrefetchScalarGridSpec(
        num_scalar_prefetch=1,
        grid=(db, n_pages // pages),
        in_specs=([per_b(n_heads, HEAD_DIM), per_b(n_heads, HEAD_DIM), per_b(HEAD_DIM, n_heads),
                   per_b(HEAD_DIM, n_heads), per_b(n_heads, 1), pl.BlockSpec((page, page), lambda bi, st, pt: (0, 0))]
                  + [kv_spec(i) for i in range(pages)] + [kv_spec(i) for i in range(pages)]
                  + [lf_spec(i) for i in range(pages)]),
        out_specs=per_b(n_heads, HEAD_DIM, 1),
        scratch_shapes=[pltpu.VMEM((n_heads, HEAD_DIM, page), F32), pltpu.VMEM((n_heads, 1), F32),
                        pltpu.VMEM((n_heads, 1), F32), pltpu.VMEM((n_heads, HEAD_DIM, page), F32),
                        pltpu.VMEM((n_heads, 1), F32)])
    return pl.pallas_call(
        functools.partial(_dec_kernel, n_heads=n_heads, pages=pages),
        grid_spec=grid_spec,
        out_shape=jax.ShapeDtypeStruct((db, n_heads, HEAD_DIM, 1), F32),
        compiler_params=_params("arbitrary", "arbitrary"),
        name="dec",
    )(pt_flat, q, kn, qt, vnt, lfn, sli, *([kt] * pages), *([vt] * pages), *([lft] * pages))


def _post_kernel(x_ref, o_ref, ys_ref, mod_ref, wglu_ref, bglu_ref, gatt_ref, gssm_ref, wout_ref, gpm_ref,
                 gpf_ref, wgu_ref, wdn_ref, gpo_ref, y_ref, *, n_heads, ff_chunk):
    x = x_ref[0]
    pairs = []
    for j in range(n_heads // 2):
        even = o_ref[0, 2 * j].astype(F32)
        odd = o_ref[0, 2 * j + 1].astype(F32)
        pairs.append(even + pltpu.roll(odd, HEAD_DIM, 1))
    attn = jnp.concatenate(pairs, axis=1)
    gl = _gelu_tanh(ys_ref[0])
    gl = gl * _sigmoid(jnp.dot(gl.astype(BF16), wglu_ref[...], preferred_element_type=F32) + bglu_ref[...])
    mix = jnp.concatenate([_rms(attn) * gatt_ref[...], _rms(gl) * gssm_ref[...]], axis=1)
    mo = jnp.dot(mix.astype(BF16), wout_ref[...], preferred_element_type=F32)
    x1 = x + mod_ref[0, 2] * (_rms(mo) * gpm_ref[...])
    h2 = (_rms(x1) * gpf_ref[...] * (1.0 + mod_ref[0, 4]) + mod_ref[0, 3]).astype(BF16)
    d_ff = wdn_ref.shape[0]
    acc = jnp.zeros_like(x)
    for c0 in range(0, d_ff, ff_chunk):
        gate = jnp.dot(h2, wgu_ref[:, c0:c0 + ff_chunk], preferred_element_type=F32)
        up = jnp.dot(h2, wgu_ref[:, d_ff + c0:d_ff + c0 + ff_chunk], preferred_element_type=F32)
        acc = acc + jnp.dot((_silu(gate) * up).astype(BF16), wdn_ref[c0:c0 + ff_chunk, :],
                            preferred_element_type=F32)
    y_ref[0] = x1 + mod_ref[0, 5] * (_rms(acc) * gpo_ref[...])


def _post_call(x, o, ys, mod, w_glu, b_glu, g_attn, g_ssm, w_out, g_post_mix, g_pre_ffn, w_gate_up, w_down,
               g_post_ffn, tm, ff_chunk):
    nb, s, d = x.shape
    n_heads = o.shape[1]
    d_ssm = ys.shape[2]
    mod_rows = mod.shape[2]
    const = lambda a: pl.BlockSpec(a.shape, lambda bi, ti: (0,) * a.ndim, pipeline_mode=pl.Buffered(1))
    rows = lambda width: pl.BlockSpec((1, tm, width), lambda bi, ti: (bi, ti, 0))
    mod_spec = (pl.BlockSpec((1, 6, 1, d), lambda bi, ti: (bi, 0, 0, 0)) if mod_rows == 1
                else pl.BlockSpec((1, 6, tm, d), lambda bi, ti: (bi, 0, ti, 0)))
    weights = (w_glu, b_glu, g_attn, g_ssm, w_out, g_post_mix, g_pre_ffn, w_gate_up, w_down, g_post_ffn)
    return pl.pallas_call(
        functools.partial(_post_kernel, n_heads=n_heads, ff_chunk=ff_chunk),
        grid=(nb, s // tm),
        in_specs=[rows(d), pl.BlockSpec((1, n_heads, tm, LANES), lambda bi, ti: (bi, 0, ti, 0)), rows(d_ssm),
                  mod_spec] + [const(w) for w in weights],
        out_specs=rows(d),
        out_shape=jax.ShapeDtypeStruct((nb, s, d), F32),
        compiler_params=_params("arbitrary", "arbitrary"),
        name="post",
    )(x, o, ys, mod, *weights)


def _row(v):
    return v.reshape(1, -1).astype(F32)


def _layer(xp, xs, cp, cs, cache_k, cache_v, cache_logf, h_re, h_im, page_table, w):
    b, s, d = xp.shape
    db = xs.shape[0]
    n_heads = w["b_f"].shape[0]
    d_attn = n_heads * HEAD_DIM
    d_ff = w["w_down"].shape[0]
    g, p = w["a_re"].shape
    d_ssm = g * SSM_GROUP

    w_in = w["w_in"]
    w_qkv = w_in[:, :3 * d_attn].astype(BF16)
    w_f = jnp.pad(w_in[:, 3 * d_attn:3 * d_attn + n_heads].astype(F32), ((0, 0), (0, LANES - n_heads)))
    b_f = jnp.pad(_row(w["b_f"]), ((0, 0), (0, LANES - n_heads)))
    w_u = w_in[:, 3 * d_attn + n_heads:].astype(BF16)
    post_w = (w["w_glu"].astype(BF16), _row(w["b_glu"]), _row(w["g_attn_out"]), _row(w["g_ssm_out"]),
              w["w_out"].astype(BF16), _row(w["g_post_mix"]), _row(w["g_pre_ffn"]), w["w_gate_up"].astype(BF16),
              w["w_down"].astype(BF16), _row(w["g_post_ffn"]))
    g_pre = _row(w["g_pre_mix"])

    n_cond = b + db
    pad_rows = -n_cond % 8
    c_all = jnp.pad(jnp.concatenate([cp, cs], axis=0).astype(F32), ((0, pad_rows), (0, 0)))
    mod = _mod_call(c_all, w["w_ada"].astype(F32), _row(w["b_ada"]))
    mod_p = mod[:b].reshape(b, 6, 1, d)
    mod_s = mod[b:n_cond].reshape(db, 6, d).transpose(1, 0, 2)[None]

    tm = min(512, s)
    k_p, v_p, lf_p, qa, ka, va, u_p = _pre_prompt_call(xp, mod_p, g_pre, w_qkv, w_f, b_f, w_u, n_heads, tm)
    o_p = _flash_call(qa, ka, va, min(512, s))
    lam_dt, lam_bar, b_bar, c = _ssm_discretize(w["a_re"], w["a_im"], w["log_dt"], w["b_re"], w["b_im"],
                                                w["c_re"], w["c_im"])
    toep, to_state, from_state, dec_re, dec_im = _ssm_chunk_operators(lam_dt, b_bar, c, SSM_CHUNK)
    nt = d_ssm // LANES
    ys_p, hfin = _ssm_call(u_p, toep, to_state, from_state, dec_re, dec_im,
                           w["d_skip"].astype(F32).reshape(nt, 1, LANES), min(2048, s))
    y_p = _post_call(xp, o_p, ys_p, mod_p, *post_w, tm=tm, ff_chunk=d_ff // 2)
    half = hfin.shape[-1] // 2
    hre_p = hfin[:, :, 0, :half].transpose(1, 0, 2).reshape(b, g, p)
    him_p = hfin[:, :, 0, half:].transpose(1, 0, 2).reshape(b, g, p)

    xs2 = xs.reshape(db, d)
    z_s, lf_s, u_s = _pre_sample_call(xs2, mod_s, g_pre, w_qkv, w_f, b_f, w_u)
    q_s = z_s[:, :d_attn].reshape(db, n_heads, HEAD_DIM)
    k_s = z_s[:, d_attn:2 * d_attn].reshape(db, n_heads, HEAD_DIM)
    v_s = z_s[:, 2 * d_attn:].reshape(db, n_heads, HEAD_DIM)
    lfn = lf_s[:, :n_heads]
    n_pages = page_table.shape[1]
    o_s = _dec_call(page_table, q_s, k_s, q_s.transpose(0, 2, 1), v_s.transpose(0, 2, 1),
                    lfn.reshape(db, n_heads, 1), cache_k, cache_v, cache_logf, math.gcd(DEC_PAGES, n_pages))
    o_s = o_s.reshape(db, n_heads, HEAD_DIM)
    o_s = jnp.pad(o_s.transpose(1, 0, 2), ((0, 0), (0, 0), (0, LANES - HEAD_DIM))).astype(BF16)[None]
    ys_s, hre_s, him_s = _ssm_step_call(u_s, h_re, h_im, lam_bar, b_bar, c, w["d_skip"].astype(F32))
    y_s = _post_call(xs2[None], o_s, ys_s[None], mod_s, *post_w, tm=db, ff_chunk=d_ff // 2)

    return dict(
        y_p=y_p, y_s=y_s.reshape(db, 1, d),
        k_p=k_p.transpose(0, 1, 4, 2, 3), v_p=v_p.transpose(0, 1, 4, 2, 3),
        f_p=lf_p.transpose(0, 1, 3, 2), r_p=hre_p, i_p=him_p,
        k_s=k_s.reshape(db, 1, n_heads, HEAD_DIM), v_s=v_s.reshape(db, 1, n_heads, HEAD_DIM),
        f_s=lfn.reshape(db, 1, n_heads), r_s=hre_s.reshape(db, g, p), i_s=him_s.reshape(db, g, p))


def kernel(x_prompt, x_sample, c_prompt, c_sample, cache_k, cache_v, cache_logf, state_ssm_re, state_ssm_im,
           page_table, w_ada, b_ada, g_pre_mix, g_post_mix, g_pre_ffn, g_post_ffn, w_in, b_f, a_re, a_im,
           log_dt, b_re, b_im, c_re, c_im, d_skip, w_glu, b_glu, g_attn_out, g_ssm_out, w_out, w_gate_up, w_down):
    depth = w_in.shape[0]
    assert depth == 1 and x_sample.shape[1] == 1, "single layer, one new token per sequence"
    weights = dict(w_ada=w_ada, b_ada=b_ada, g_pre_mix=g_pre_mix, g_post_mix=g_post_mix, g_pre_ffn=g_pre_ffn,
                   g_post_ffn=g_post_ffn, w_in=w_in, b_f=b_f, a_re=a_re, a_im=a_im, log_dt=log_dt, b_re=b_re,
                   b_im=b_im, c_re=c_re, c_im=c_im, d_skip=d_skip, w_glu=w_glu, b_glu=b_glu,
                   g_attn_out=g_attn_out, g_ssm_out=g_ssm_out, w_out=w_out, w_gate_up=w_gate_up, w_down=w_down)
    w0 = {name: val[0] for name, val in weights.items()}
    r = _layer(x_prompt.astype(F32), x_sample.astype(F32), c_prompt, c_sample, cache_k, cache_v, cache_logf,
               state_ssm_re[0], state_ssm_im[0], page_table, w0)
    stack = lambda a: a[None]
    return (r["y_p"].astype(x_prompt.dtype), r["y_s"].astype(x_sample.dtype),
            stack(r["k_p"]), stack(r["v_p"]), stack(r["f_p"]), stack(r["r_p"]), stack(r["i_p"]),
            stack(r["k_s"]), stack(r["v_s"]), stack(r["f_s"]), stack(r["r_s"]), stack(r["i_s"]))
```

```python
import functools
import math

import numpy as np
import jax
import jax.numpy as jnp
from jax import lax
from jax.experimental import pallas as pl
from jax.experimental.pallas import tpu as pltpu

F32 = jnp.float32
BF16 = jnp.bfloat16
HI = lax.Precision.HIGHEST
EPS = 1e-6
NEG = -1e30
HEAD_DIM = 64
SSM_GROUP = 16
LANES = 128
GROUPS_PER_TILE = LANES // SSM_GROUP
VMEM_LIMIT = 56 * 1024 * 1024
SSM_CHUNK = 8
V_ROWS = 80
SQRT_2_OVER_PI = math.sqrt(2.0 / math.pi)
LOG2_E = math.log2(math.e)


def _sigmoid(x):
    return 1.0 / (1.0 + jnp.exp(-x))


def _silu(x):
    return x * _sigmoid(x)


def _log_sigmoid(x):
    return jnp.minimum(x, 0.0) - jnp.log1p(jnp.exp(-jnp.abs(x)))


def _gelu_tanh(x):
    return x * (0.5 * (1.0 + jnp.tanh(SQRT_2_OVER_PI * (x + 0.044715 * (x * x * x)))))


def _rms(x):
    return x * lax.rsqrt(jnp.mean(x * x, axis=-1, keepdims=True) + EPS)


def _params(*sem):
    return pltpu.CompilerParams(dimension_semantics=sem, vmem_limit_bytes=VMEM_LIMIT)


def _mod_kernel(c_ref, w_ref, b_ref, o_ref):
    c = c_ref[...]
    o_ref[...] = jnp.dot(_silu(c), w_ref[...], precision=HI, preferred_element_type=F32) + b_ref[...]


def _mod_call(c_all, w_ada, b_ada):
    rows, d = c_all.shape
    n = w_ada.shape[1]
    return pl.pallas_call(
        _mod_kernel,
        grid=(n // d,),
        in_specs=[pl.BlockSpec((rows, d), lambda i: (0, 0)),
                  pl.BlockSpec((d, d), lambda i: (0, i)),
                  pl.BlockSpec((1, d), lambda i: (0, i))],
        out_specs=pl.BlockSpec((rows, d), lambda i: (0, i)),
        out_shape=jax.ShapeDtypeStruct((rows, n), F32),
        compiler_params=_params("arbitrary"),
        name="mod",
    )(c_all, w_ada, b_ada)


def _split3(f):
    hi = f.astype(BF16)
    r1 = f - hi.astype(F32)
    mid = r1.astype(BF16)
    lo = (r1 - mid.astype(F32)).astype(BF16)
    return hi, mid, lo


def _pre_prompt_kernel(x_ref, mod_ref, g_ref, wqkv_ref, wfh_ref, wfl_ref, bf_ref, wu_ref, tri_ref, pq_ref, pk_ref,
                       cq_ref, ck_ref, kt_ref, vt_ref, lft_ref, qa_ref, ka_ref, va_ref, u_ref, carry_ref,
                       *, n_heads, d_attn):
    @pl.when(pl.program_id(1) == 0)
    def _():
        carry_ref[...] = jnp.zeros_like(carry_ref)

    x = x_ref[0]
    tm = x.shape[0]
    h = _rms(x) * g_ref[...] * (1.0 + mod_ref[0, 1]) + mod_ref[0, 0]
    hb = h.astype(BF16)
    z = jnp.dot(hb, wqkv_ref[...], preferred_element_type=F32)
    u_ref[0] = jnp.dot(hb, wu_ref[...], preferred_element_type=F32)

    h_lo = (h - hb.astype(F32)).astype(BF16)
    fl = (jnp.dot(hb, wfh_ref[...], preferred_element_type=F32) + jnp.dot(hb, wfl_ref[...], preferred_element_type=F32)
          + jnp.dot(h_lo, wfh_ref[...], preferred_element_type=F32) + bf_ref[...])
    lane = lax.broadcasted_iota(jnp.int32, (tm, LANES), 1)
    logf = jnp.where(lane < n_heads, _log_sigmoid(fl), 0.0)
    logf_t = logf.T
    for pg in range(tm // LANES):
        lft_ref[0, pg] = logf_t[:n_heads, pg * LANES:(pg + 1) * LANES]
    tri_sum = jnp.dot(tri_ref[...], jnp.concatenate(_split3(logf), axis=1), preferred_element_type=F32)
    cum = tri_sum[:, :LANES] + tri_sum[:, LANES:2 * LANES] + tri_sum[:, 2 * LANES:] + carry_ref[...]
    carry_ref[...] = cum[tm - 1:tm, :]

    fs = jnp.concatenate(_split3(cum * LOG2_E), axis=1)
    augq = jnp.dot(fs, pq_ref[...], preferred_element_type=F32) + cq_ref[...]
    augk = jnp.dot(fs, pk_ref[...], preferred_element_type=F32) + ck_ref[...]
    low = lane < HEAD_DIM
    vone = jnp.where(lane == HEAD_DIM, 1.0, 0.0)
    scale = HEAD_DIM ** -0.5 * LOG2_E
    for j in range(n_heads // 2):
        zq = z[:, j * LANES:(j + 1) * LANES] * scale
        zk = z[:, d_attn + j * LANES:d_attn + (j + 1) * LANES]
        zv = z[:, 2 * d_attn + j * LANES:2 * d_attn + (j + 1) * LANES]
        for par in range(2):
            hh = 2 * j + par
            if par:
                zq, zk, zv = (pltpu.roll(a, HEAD_DIM, 1) for a in (zq, zk, zv))
            qa_ref[0, hh] = jnp.where(low, zq, augq[:, hh * LANES:(hh + 1) * LANES]).astype(BF16)
            ka_ref[0, hh] = jnp.where(low, zk, augk[:, hh * LANES:(hh + 1) * LANES]).astype(BF16)
            k_t = zk.T
            v_t = jnp.where(low, zv, vone).T
            va_ref[0, hh, 0] = v_t[:V_ROWS].astype(BF16)
            for pg in range(tm // LANES):
                kt_ref[0, pg, hh] = k_t[:HEAD_DIM, pg * LANES:(pg + 1) * LANES]
                vt_ref[0, pg, hh] = v_t[:HEAD_DIM, pg * LANES:(pg + 1) * LANES]


def _aug_constants(n_heads):
    pq = np.zeros((3 * LANES, n_heads * LANES), np.float32)
    pk = np.zeros((3 * LANES, n_heads * LANES), np.float32)
    cq = np.zeros((1, n_heads * LANES), np.float32)
    ck = np.zeros((1, n_heads * LANES), np.float32)
    for h in range(n_heads):
        for piece in range(3):
            pq[piece * LANES + h, h * LANES + HEAD_DIM + piece] = 1.0
            pk[piece * LANES + h, h * LANES + HEAD_DIM + 3 + piece] = -1.0
            cq[0, h * LANES + HEAD_DIM + 3 + piece] = 1.0
            ck[0, h * LANES + HEAD_DIM + piece] = 1.0
    return jnp.asarray(pq, BF16), jnp.asarray(pk, BF16), jnp.asarray(cq), jnp.asarray(ck)


def _pre_prompt_call(x, mod_p, g_pre, w_qkv, w_f, b_f, w_u, n_heads, tm):
    b, s, d = x.shape
    d_attn = n_heads * HEAD_DIM
    d_ssm = w_u.shape[1]
    tri = jnp.asarray(np.tril(np.ones((tm, tm), np.float32)), BF16)
    w_f_hi = w_f.astype(BF16)
    w_f_lo = (w_f - w_f_hi.astype(F32)).astype(BF16)
    pq, pk, cq, ck = _aug_constants(n_heads)
    pages = tm // LANES
    const = lambda *shape: pl.BlockSpec(shape, lambda bi, ti: (0,) * len(shape))
    rows = lambda width: pl.BlockSpec((1, tm, width), lambda bi, ti: (bi, ti, 0))
    heads = pl.BlockSpec((1, n_heads, tm, LANES), lambda bi, ti: (bi, 0, ti, 0))
    heads_t = pl.BlockSpec((1, n_heads, 1, V_ROWS, tm), lambda bi, ti: (bi, 0, ti, 0, 0))
    paged = pl.BlockSpec((1, pages, n_heads, HEAD_DIM, LANES), lambda bi, ti: (bi, ti, 0, 0, 0))
    paged_shape = jax.ShapeDtypeStruct((b, s // LANES, n_heads, HEAD_DIM, LANES), F32)
    aug_shape = jax.ShapeDtypeStruct((b, n_heads, s, LANES), BF16)
    aug_t_shape = jax.ShapeDtypeStruct((b, n_heads, s // tm, V_ROWS, tm), BF16)
    return pl.pallas_call(
        functools.partial(_pre_prompt_kernel, n_heads=n_heads, d_attn=d_attn),
        grid=(b, s // tm),
        in_specs=[rows(d),
                  pl.BlockSpec((1, 6, 1, d), lambda bi, ti: (bi, 0, 0, 0)),
                  const(1, d), const(d, 3 * d_attn), const(d, LANES), const(d, LANES), const(1, LANES),
                  const(d, d_ssm), const(tm, tm), const(3 * LANES, n_heads * LANES),
                  const(3 * LANES, n_heads * LANES), const(1, n_heads * LANES), const(1, n_heads * LANES)],
        out_specs=[paged, paged, pl.BlockSpec((1, pages, n_heads, LANES), lambda bi, ti: (bi, ti, 0, 0)),
                   heads, heads, heads_t, rows(d_ssm)],
        out_shape=[paged_shape, paged_shape, jax.ShapeDtypeStruct((b, s // LANES, n_heads, LANES), F32),
                   aug_shape, aug_shape, aug_t_shape, jax.ShapeDtypeStruct((b, s, d_ssm), F32)],
        scratch_shapes=[pltpu.VMEM((1, LANES), F32)],
        compiler_params=_params("arbitrary", "arbitrary"),
        name="pre_prompt",
    )(x, mod_p, g_pre, w_qkv, w_f_hi, w_f_lo, b_f, w_u, tri, pq, pk, cq, ck)


def _pre_sample_kernel(x_ref, mod_ref, g_ref, wqkv_ref, wf_ref, bf_ref, wu_ref, z_ref, lf_ref, u_ref):
    x = x_ref[...]
    h = _rms(x) * g_ref[...] * (1.0 + mod_ref[0, 1]) + mod_ref[0, 0]
    hb = h.astype(BF16)
    z_ref[...] = jnp.dot(hb, wqkv_ref[...], preferred_element_type=F32)
    u_ref[...] = jnp.dot(hb, wu_ref[...], preferred_element_type=F32)
    fl = jnp.dot(h, wf_ref[...], precision=HI, preferred_element_type=F32) + bf_ref[...]
    lf_ref[...] = _log_sigmoid(fl)


def _pre_sample_call(x, mod_s, g_pre, w_qkv, w_f, b_f, w_u):
    rows = x.shape[0]
    return pl.pallas_call(
        _pre_sample_kernel,
        out_shape=[jax.ShapeDtypeStruct((rows, w_qkv.shape[1]), F32),
                   jax.ShapeDtypeStruct((rows, LANES), F32),
                   jax.ShapeDtypeStruct((rows, w_u.shape[1]), F32)],
        compiler_params=pltpu.CompilerParams(vmem_limit_bytes=VMEM_LIMIT),
        name="pre_sample",
    )(x, mod_s, g_pre, w_qkv, w_f, b_f, w_u)


FLASH_HEADS = 4


def _flash_kernel(q_ref, k_ref, vt_ref, o_ref, st_ref, bmax_ref, m_ref, acc_ref, *, tq):
    qi = pl.program_id(2)
    heads = q_ref.shape[1]

    def put_scores(kb, buf):
        off = pl.multiple_of(kb * tq, tq)
        for h in range(heads):
            st = lax.dot_general(k_ref[0, h, pl.ds(off, tq), :], q_ref[0, h], (((1,), (1,)), ((), ())),
                                 preferred_element_type=F32)
            st_ref[h, buf] = st
            bmax_ref[h, buf] = jnp.max(st, axis=0, keepdims=True)

    def update(kb, buf, diagonal):
        for h in range(heads):
            st = st_ref[h, buf]
            if diagonal:
                key = lax.broadcasted_iota(jnp.int32, st.shape, 0)
                qry = lax.broadcasted_iota(jnp.int32, st.shape, 1)
                st = jnp.where(key <= qry, st, NEG)
                block_max = jnp.max(st, axis=0, keepdims=True)
            else:
                block_max = bmax_ref[h, buf]
            m = m_ref[h]
            m_new = jnp.maximum(m, block_max)
            acc_ref[h] = acc_ref[h] * jnp.exp2(m - m_new) + jnp.dot(
                vt_ref[0, h, kb], jnp.exp2(st - m_new).astype(BF16), preferred_element_type=F32)
            m_ref[h] = m_new

    def finish():
        for h in range(heads):
            acc = acc_ref[h]
            o = acc[:HEAD_DIM] * (1.0 / acc[HEAD_DIM:HEAD_DIM + 1, :])
            o_ref[0, h] = jnp.concatenate([o, jnp.zeros((LANES - HEAD_DIM, tq), F32)], axis=0).T.astype(BF16)

    m_ref[...] = jnp.full_like(m_ref, NEG)
    acc_ref[...] = jnp.zeros_like(acc_ref)
    put_scores(0, 0)

    def pair(j, carry):
        put_scores(2 * j + 1, 1)
        update(2 * j, 0, False)
        put_scores(2 * j + 2, 0)
        update(2 * j + 1, 1, False)
        return carry

    lax.fori_loop(0, qi // 2, pair, 0)

    @pl.when(qi % 2 == 1)
    def _():
        put_scores(qi, 1)
        update(qi - 1, 0, False)
        update(qi, 1, True)
        finish()

    @pl.when(qi % 2 == 0)
    def _():
        update(qi, 0, True)
        finish()


def _flash_call(qa, ka, vat, tq):
    b, h, s, _ = qa.shape
    assert vat.shape[-1] == tq, "value chunks are laid out per key block"
    nh = math.gcd(FLASH_HEADS, h)
    qspec = pl.BlockSpec((1, nh, tq, LANES), lambda bi, hi, qi: (bi, hi, qi, 0))
    once = pl.Buffered(1)
    kspec = pl.BlockSpec((1, nh, s, LANES), lambda bi, hi, qi: (bi, hi, 0, 0), pipeline_mode=once)
    vspec = pl.BlockSpec((1, nh, s // tq, V_ROWS, tq), lambda bi, hi, qi: (bi, hi, 0, 0, 0), pipeline_mode=once)
    return pl.pallas_call(
        functools.partial(_flash_kernel, tq=tq),
        grid=(b, h // nh, s // tq),
        in_specs=[qspec, kspec, vspec],
        out_specs=qspec,
        out_shape=jax.ShapeDtypeStruct((b, h, s, LANES), BF16),
        scratch_shapes=[pltpu.VMEM((nh, 2, tq, tq), F32), pltpu.VMEM((nh, 2, 1, tq), F32),
                        pltpu.VMEM((nh, 1, tq), F32),
                        pltpu.VMEM((nh, V_ROWS, tq), F32)],
        compiler_params=_params("arbitrary", "arbitrary", "arbitrary"),
        name="flash",
    )(qa, ka, vat)


def _cmul(ar, ai, br, bi):
    return ar * br - ai * bi, ar * bi + ai * br


def _ssm_discretize(a_re, a_im, log_dt, b_re, b_im, c_re, c_im):
    ar, ai = a_re.astype(F32), a_im.astype(F32)
    dt = jnp.exp(log_dt.astype(F32))[:, None]
    lam_dt = (ar * dt, ai * dt)
    mag = jnp.exp(lam_dt[0])
    lam_bar = (mag * jnp.cos(lam_dt[1]), mag * jnp.sin(lam_dt[1]))
    den = ar * ar + ai * ai
    nr, ni = lam_bar[0] - 1.0, lam_bar[1]
    coef = ((nr * ar + ni * ai) / den, (ni * ar - nr * ai) / den)
    b_bar = _cmul(coef[0][..., None], coef[1][..., None], b_re.astype(F32), b_im.astype(F32))
    c = (c_re.astype(F32), c_im.astype(F32))
    return lam_dt, lam_bar, b_bar, c


def _ssm_chunk_operators(lam_dt, b_bar, c, chunk):
    g, p = lam_dt[0].shape
    nt = g // GROUPS_PER_TILE
    steps = jnp.arange(chunk + 1, dtype=F32)[:, None, None]
    mag = jnp.exp(steps * lam_dt[0][None])
    pw = (mag * jnp.cos(steps * lam_dt[1][None]), mag * jnp.sin(steps * lam_dt[1][None]))
    gpt = GROUPS_PER_TILE
    group_of = lambda cols, width: jnp.asarray(
        (np.arange(cols)[None, :] // width) % gpt == np.arange(gpt)[:, None], F32)
    pb = _cmul(pw[0][:chunk, :, :, None], pw[1][:chunk, :, :, None], b_bar[0][None], b_bar[1][None])
    kmat = (jnp.einsum("ghp,dgpk->dkgh", c[0], pb[0], precision=HI)
            - jnp.einsum("ghp,dgpk->dkgh", c[1], pb[1], precision=HI))
    kc = kmat.reshape(chunk, SSM_GROUP, nt, LANES).transpose(2, 0, 1, 3)
    lag_blocks = (kc[:, :, None] * group_of(LANES, SSM_GROUP)[None, None, :, None, :]
                  ).reshape(nt, chunk, LANES, LANES).astype(BF16)
    rev = chunk - 1 - np.arange(chunk)
    wb = _cmul(pw[0][rev][..., None], pw[1][rev][..., None], b_bar[0][None], b_bar[1][None])
    wsm = jnp.stack([part.reshape(chunk, nt, gpt, p, SSM_GROUP) for part in wb])
    wsm = wsm.transpose(2, 1, 5, 0, 3, 4).reshape(nt, chunk, SSM_GROUP, 2 * gpt * p)
    to_state = (wsm[:, :, None] * group_of(2 * gpt * p, p)[None, None, :, None, :]
                ).reshape(nt, chunk * LANES, 2 * gpt * p).astype(BF16)
    cp = _cmul(c[0][None], c[1][None], pw[0][1:chunk + 1][:, :, None, :], pw[1][1:chunk + 1][:, :, None, :])
    csm = jnp.stack([part.reshape(chunk, nt, gpt, SSM_GROUP, p) for part in (cp[0], -cp[1])])
    csm = csm.transpose(2, 0, 5, 1, 3, 4).reshape(nt, 2, p, chunk * LANES)
    from_state = (csm[:, :, None] * group_of(chunk * LANES, SSM_GROUP)[None, None, :, None, :]
                  ).reshape(nt, 2 * gpt * p, chunk * LANES).astype(BF16)
    dec_re = pw[0][chunk].reshape(nt, 1, gpt * p)
    dec_im = pw[1][chunk].reshape(nt, 1, gpt * p)
    return lag_blocks, to_state, from_state, dec_re, dec_im


def _ssm_kernel(u_ref, lag_ref, g_ref, c_ref, are_ref, aim_ref, d_ref, y_ref, hfin_ref, xin_ref, xs_ref, st_ref,
                toep_ref, *, chunk, n_chunks):
    r = pl.program_id(2)

    @pl.when((pl.program_id(1) == 0) & (r == 0))
    def _():
        for l_in in range(chunk):
            for l_out in range(chunk):
                blk = lag_ref[0, l_out - l_in] if l_out >= l_in else jnp.zeros((LANES, LANES), BF16)
                toep_ref[l_in * LANES:(l_in + 1) * LANES, l_out * LANES:(l_out + 1) * LANES] = blk

    @pl.when(r == 0)
    def _():
        st_ref[...] = jnp.zeros_like(st_ref)

    half = st_ref.shape[1] // 2
    pieces = [u_ref[0, pl.ds(l, n_chunks, stride=chunk), :] for l in range(chunk)]
    u2 = jnp.concatenate([pc.astype(BF16) for pc in pieces], axis=1)
    xin_ref[...] = jnp.dot(u2, g_ref[0], preferred_element_type=F32)
    ar = are_ref[0]
    ai = aim_ref[0]

    def body(i, carry):
        xr, xi = carry
        base = pl.multiple_of(i * 8, 8)
        blk = xin_ref[pl.ds(base, 8), :]
        rows_r, rows_i = [], []
        for rr in range(8):
            rows_r.append(xr)
            rows_i.append(xi)
            xr, xi = (ar * xr - ai * xi + blk[rr:rr + 1, :half],
                      ar * xi + ai * xr + blk[rr:rr + 1, half:])
        xs_ref[pl.ds(base, 8), :] = jnp.concatenate(
            [jnp.concatenate(rows_r, axis=0), jnp.concatenate(rows_i, axis=0)], axis=1)
        return xr, xi

    xr, xi = lax.fori_loop(0, n_chunks // 8, body, (st_ref[:, :half], st_ref[:, half:]), unroll=True)
    st_ref[...] = jnp.concatenate([xr, xi], axis=1)
    y2 = (jnp.dot(u2, toep_ref[...], preferred_element_type=F32)
          + jnp.dot(xs_ref[...].astype(BF16), c_ref[0], preferred_element_type=F32))
    dd = d_ref[0]
    for l in range(chunk):
        y_ref[0, pl.ds(l, n_chunks, stride=chunk), :] = y2[:, l * LANES:(l + 1) * LANES] + dd * pieces[l]

    @pl.when(r == pl.num_programs(2) - 1)
    def _():
        hfin_ref[0, 0] = st_ref[...]


def _ssm_call(u, lag_blocks, to_state, from_state, dec_re, dec_im, d_skip, rows_per_step):
    b, s, d_ssm = u.shape
    nt = d_ssm // LANES
    chunk = lag_blocks.shape[1]
    n_chunks = rows_per_step // chunk
    nstate = to_state.shape[2]
    tile = lambda *shape: pl.BlockSpec((1,) + shape, lambda j, bi, r: (j, 0, 0))
    useq = pl.BlockSpec((1, rows_per_step, LANES), lambda j, bi, r: (bi, r, j))
    return pl.pallas_call(
        functools.partial(_ssm_kernel, chunk=chunk, n_chunks=n_chunks),
        grid=(nt, b, s // rows_per_step),
        in_specs=[useq, pl.BlockSpec((1, chunk, LANES, LANES), lambda j, bi, r: (j, 0, 0, 0)),
                  tile(chunk * LANES, nstate),
                  tile(nstate, chunk * LANES), tile(1, nstate // 2), tile(1, nstate // 2), tile(1, LANES)],
        out_specs=[useq, pl.BlockSpec((1, 1, 1, nstate), lambda j, bi, r: (j, bi, 0, 0))],
        out_shape=[jax.ShapeDtypeStruct((b, s, d_ssm), F32), jax.ShapeDtypeStruct((nt, b, 1, nstate), F32)],
        scratch_shapes=[pltpu.VMEM((n_chunks, nstate), F32), pltpu.VMEM((n_chunks, nstate), F32),
                        pltpu.VMEM((1, nstate), F32), pltpu.VMEM((chunk * LANES, chunk * LANES), BF16)],
        compiler_params=_params("arbitrary", "arbitrary", "arbitrary"),
        name="ssm",
    )(u, lag_blocks, to_state, from_state, dec_re, dec_im, d_skip)


def _ssm_step_kernel(u_ref, hre_ref, him_ref, bre_ref, bim_ref, lre_ref, lim_ref, cre_ref, cim_ref, d_ref,
                     y_ref, xre_ref, xim_ref):
    u = u_ref[...]
    hr, hi = hre_ref[...], him_ref[...]
    lr, li = lre_ref[...], lim_ref[...]
    xr = lr * hr - li * hi + jnp.dot(u, bre_ref[...], precision=HI, preferred_element_type=F32)
    xi = lr * hi + li * hr + jnp.dot(u, bim_ref[...], precision=HI, preferred_element_type=F32)
    xre_ref[...] = xr
    xim_ref[...] = xi
    y_ref[...] = (jnp.dot(xr, cre_ref[...], precision=HI, preferred_element_type=F32)
                  - jnp.dot(xi, cim_ref[...], precision=HI, preferred_element_type=F32) + d_ref[...] * u)


def _ssm_step_call(u, h_re, h_im, lam_bar, b_bar, c, d_skip):
    rows, d_ssm = u.shape
    g, p = lam_bar[0].shape
    eye = jnp.eye(g, dtype=F32)
    bmat = [jnp.einsum("gpk,gf->gkfp", part, eye, precision=HI).reshape(d_ssm, g * p) for part in b_bar]
    cmat = [jnp.einsum("ghp,gf->gpfh", part, eye, precision=HI).reshape(g * p, d_ssm) for part in c]
    return pl.pallas_call(
        _ssm_step_kernel,
        out_shape=[jax.ShapeDtypeStruct((rows, d_ssm), F32), jax.ShapeDtypeStruct((rows, g * p), F32),
                   jax.ShapeDtypeStruct((rows, g * p), F32)],
        compiler_params=pltpu.CompilerParams(vmem_limit_bytes=VMEM_LIMIT),
        name="ssm_step",
    )(u, h_re.reshape(rows, g * p), h_im.reshape(rows, g * p), bmat[0], bmat[1],
      lam_bar[0].reshape(1, g * p), lam_bar[1].reshape(1, g * p), cmat[0], cmat[1], d_skip.reshape(1, d_ssm))


DEC_PAGES = 16


def _dec_kernel(pt_ref, q_ref, kn_ref, qt_ref, vnt_ref, lfn_ref, sli_ref, *rest, n_heads, pages):
    del pt_ref
    kt_refs, vt_refs, lf_refs = rest[:pages], rest[pages:2 * pages], rest[2 * pages:3 * pages]
    o_ref, qb_ref, m_ref, l_ref, acc_ref, carry_ref = rest[3 * pages:]
    step = pl.program_id(1)
    scale = HEAD_DIM ** -0.5

    @pl.when(step == 0)
    def _():
        qt = qt_ref[0] * scale
        vnt = vnt_ref[0]
        lane = lax.broadcasted_iota(jnp.int32, (HEAD_DIM, LANES), 1)
        for h in range(n_heads):
            qb_ref[h] = jnp.broadcast_to(qt[:, h:h + 1], (HEAD_DIM, LANES))
            acc_ref[h] = jnp.where(lane == 0, jnp.broadcast_to(vnt[:, h:h + 1], (HEAD_DIM, LANES)), 0.0)
        m_ref[...] = jnp.sum(q_ref[0] * kn_ref[0], axis=1, keepdims=True) * scale
        l_ref[...] = jnp.ones_like(l_ref)
        carry_ref[...] = lfn_ref[0]

    lf_all = jnp.concatenate([r[0, 0] for r in lf_refs], axis=0)
    sli = sli_ref[...]
    incl = sum(jnp.dot(piece, sli, preferred_element_type=F32) for piece in _split3(lf_all))
    run = carry_ref[...]
    scores = [None] * pages
    for i in reversed(range(pages)):
        rows = slice(i * n_heads, (i + 1) * n_heads)
        bias = incl[rows] - lf_all[rows] + run
        run = run + incl[rows][:, 0:1]
        qk = [jnp.sum(kt_refs[i][0, 0, h] * qb_ref[h], axis=0, keepdims=True) for h in range(n_heads)]
        scores[i] = jnp.concatenate(qk, axis=0) + bias
    carry_ref[...] = run

    m_old = m_ref[...]
    m_new = jnp.maximum(m_old, jnp.max(functools.reduce(jnp.maximum, scores), axis=1, keepdims=True))
    corr = jnp.exp(m_old - m_new)
    probs = [jnp.exp(s - m_new) for s in scores]
    l_ref[...] = l_ref[...] * corr + jnp.sum(functools.reduce(jnp.add, probs), axis=1, keepdims=True)
    m_ref[...] = m_new
    for h in range(n_heads):
        a = acc_ref[h] * corr[h:h + 1, :]
        for i in range(pages):
            a = a + vt_refs[i][0, 0, h] * probs[i][h:h + 1, :]
        acc_ref[h] = a

    @pl.when(step == pl.num_programs(1) - 1)
    def _():
        inv = 1.0 / l_ref[...]
        for h in range(n_heads):
            o_ref[0, h] = jnp.sum(acc_ref[h], axis=1, keepdims=True) * inv[h:h + 1, :]


def _dec_call(page_table, q, kn, qt, vnt, lfn, cache_k, cache_v, cache_logf, pages):
    db, n_pages = page_table.shape
    _, _, page, n_heads, _ = cache_k.shape
    assert page == LANES and n_pages % pages == 0
    kt = jnp.transpose(cache_k, (0, 1, 3, 4, 2))
    vt = jnp.transpose(cache_v, (0, 1, 3, 4, 2))
    lft = jnp.transpose(cache_logf, (0, 1, 3, 2))
    sli = jnp.asarray(np.tril(np.ones((page, page), np.float32)), BF16)
    pt_flat = page_table.reshape(-1)

    def phys(i):
        return lambda bi, st, pt: pt[bi * n_pages + n_pages - pages * (st + 1) + i]

    per_b = lambda *shape: pl.BlockSpec((1,) + shape, lambda bi, st, pt: (bi,) + (0,) * len(shape))
    kv_spec = lambda i: pl.BlockSpec((1, 1, n_heads, HEAD_DIM, page),
                                     lambda bi, st, pt, f=phys(i): (0, f(bi, st, pt), 0, 0, 0))
    lf_spec = lambda i: pl.BlockSpec((1, 1, n_heads, page), lambda bi, st, pt, f=phys(i): (0, f(bi, st, pt), 0, 0))
    grid_spec = pltpu.PrefetchScalarGridSpec(
        num_scalar_prefetch=1,
        grid=(db, n_pages // pages),
        in_specs=([per_b(n_heads, HEAD_DIM), per_b(n_heads, HEAD_DIM), per_b(HEAD_DIM, n_heads),
                   per_b(HEAD_DIM, n_heads), per_b(n_heads, 1), pl.BlockSpec((page, page), lambda bi, st, pt: (0, 0))]
                  + [kv_spec(i) for i in range(pages)] + [kv_spec(i) for i in range(pages)]
                  + [lf_spec(i) for i in range(pages)]),
        out_specs=per_b(n_heads, HEAD_DIM, 1),
        scratch_shapes=[pltpu.VMEM((n_heads, HEAD_DIM, page), F32), pltpu.VMEM((n_heads, 1), F32),
                        pltpu.VMEM((n_heads, 1), F32), pltpu.VMEM((n_heads, HEAD_DIM, page), F32),
                        pltpu.VMEM((n_heads, 1), F32)])
    return pl.pallas_call(
        functools.partial(_dec_kernel, n_heads=n_heads, pages=pages),
        grid_spec=grid_spec,
        out_shape=jax.ShapeDtypeStruct((db, n_heads, HEAD_DIM, 1), F32),
        compiler_params=_params("arbitrary", "arbitrary"),
        name="dec",
    )(pt_flat, q, kn, qt, vnt, lfn, sli, *([kt] * pages), *([vt] * pages), *([lft] * pages))


def _post_kernel(x_ref, o_ref, ys_ref, mod_ref, wglu_ref, bglu_ref, gatt_ref, gssm_ref, wout_ref, gpm_ref,
                 gpf_ref, wgu_ref, wdn_ref, gpo_ref, y_ref, *, n_heads, ff_chunk):
    x = x_ref[0]
    pairs = []
    for j in range(n_heads // 2):
        even = o_ref[0, 2 * j].astype(F32)
        odd = o_ref[0, 2 * j + 1].astype(F32)
        pairs.append(even + pltpu.roll(odd, HEAD_DIM, 1))
    attn = jnp.concatenate(pairs, axis=1)
    gl = _gelu_tanh(ys_ref[0])
    gl = gl * _sigmoid(jnp.dot(gl.astype(BF16), wglu_ref[...], preferred_element_type=F32) + bglu_ref[...])
    mix = jnp.concatenate([_rms(attn) * gatt_ref[...], _rms(gl) * gssm_ref[...]], axis=1)
    mo = jnp.dot(mix.astype(BF16), wout_ref[...], preferred_element_type=F32)
    x1 = x + mod_ref[0, 2] * (_rms(mo) * gpm_ref[...])
    h2 = (_rms(x1) * gpf_ref[...] * (1.0 + mod_ref[0, 4]) + mod_ref[0, 3]).astype(BF16)
    d_ff = wdn_ref.shape[0]
    acc = jnp.zeros_like(x)
    for c0 in range(0, d_ff, ff_chunk):
        gate = jnp.dot(h2, wgu_ref[:, c0:c0 + ff_chunk], preferred_element_type=F32)
        up = jnp.dot(h2, wgu_ref[:, d_ff + c0:d_ff + c0 + ff_chunk], preferred_element_type=F32)
        acc = acc + jnp.dot((_silu(gate) * up).astype(BF16), wdn_ref[c0:c0 + ff_chunk, :],
                            preferred_element_type=F32)
    y_ref[0] = x1 + mod_ref[0, 5] * (_rms(acc) * gpo_ref[...])


def _post_call(x, o, ys, mod, w_glu, b_glu, g_attn, g_ssm, w_out, g_post_mix, g_pre_ffn, w_gate_up, w_down,
               g_post_ffn, tm, ff_chunk):
    nb, s, d = x.shape
    n_heads = o.shape[1]
    d_ssm = ys.shape[2]
    mod_rows = mod.shape[2]
    const = lambda a: pl.BlockSpec(a.shape, lambda bi, ti: (0,) * a.ndim, pipeline_mode=pl.Buffered(1))
    rows = lambda width: pl.BlockSpec((1, tm, width), lambda bi, ti: (bi, ti, 0))
    mod_spec = (pl.BlockSpec((1, 6, 1, d), lambda bi, ti: (bi, 0, 0, 0)) if mod_rows == 1
                else pl.BlockSpec((1, 6, tm, d), lambda bi, ti: (bi, 0, ti, 0)))
    weights = (w_glu, b_glu, g_attn, g_ssm, w_out, g_post_mix, g_pre_ffn, w_gate_up, w_down, g_post_ffn)
    return pl.pallas_call(
        functools.partial(_post_kernel, n_heads=n_heads, ff_chunk=ff_chunk),
        grid=(nb, s // tm),
        in_specs=[rows(d), pl.BlockSpec((1, n_heads, tm, LANES), lambda bi, ti: (bi, 0, ti, 0)), rows(d_ssm),
                  mod_spec] + [const(w) for w in weights],
        out_specs=rows(d),
        out_shape=jax.ShapeDtypeStruct((nb, s, d), F32),
        compiler_params=_params("arbitrary", "arbitrary"),
        name="post",
    )(x, o, ys, mod, *weights)


def _row(v):
    return v.reshape(1, -1).astype(F32)


def _layer(xp, xs, cp, cs, cache_k, cache_v, cache_logf, h_re, h_im, page_table, w):
    b, s, d = xp.shape
    db = xs.shape[0]
    n_heads = w["b_f"].shape[0]
    d_attn = n_heads * HEAD_DIM
    d_ff = w["w_down"].shape[0]
    g, p = w["a_re"].shape
    d_ssm = g * SSM_GROUP

    w_in = w["w_in"]
    w_qkv = w_in[:, :3 * d_attn].astype(BF16)
    w_f = jnp.pad(w_in[:, 3 * d_attn:3 * d_attn + n_heads].astype(F32), ((0, 0), (0, LANES - n_heads)))
    b_f = jnp.pad(_row(w["b_f"]), ((0, 0), (0, LANES - n_heads)))
    w_u = w_in[:, 3 * d_attn + n_heads:].astype(BF16)
    post_w = (w["w_glu"].astype(BF16), _row(w["b_glu"]), _row(w["g_attn_out"]), _row(w["g_ssm_out"]),
              w["w_out"].astype(BF16), _row(w["g_post_mix"]), _row(w["g_pre_ffn"]), w["w_gate_up"].astype(BF16),
              w["w_down"].astype(BF16), _row(w["g_post_ffn"]))
    g_pre = _row(w["g_pre_mix"])

    n_cond = b + db
    pad_rows = -n_cond % 8
    c_all = jnp.pad(jnp.concatenate([cp, cs], axis=0).astype(F32), ((0, pad_rows), (0, 0)))
    mod = _mod_call(c_all, w["w_ada"].astype(F32), _row(w["b_ada"]))
    mod_p = mod[:b].reshape(b, 6, 1, d)
    mod_s = mod[b:n_cond].reshape(db, 6, d).transpose(1, 0, 2)[None]

    tm = min(512, s)
    k_p, v_p, lf_p, qa, ka, va, u_p = _pre_prompt_call(xp, mod_p, g_pre, w_qkv, w_f, b_f, w_u, n_heads, tm)
    o_p = _flash_call(qa, ka, va, min(512, s))
    lam_dt, lam_bar, b_bar, c = _ssm_discretize(w["a_re"], w["a_im"], w["log_dt"], w["b_re"], w["b_im"],
                                                w["c_re"], w["c_im"])
    lag_blocks, to_state, from_state, dec_re, dec_im = _ssm_chunk_operators(lam_dt, b_bar, c, SSM_CHUNK)
    nt = d_ssm // LANES
    ys_p, hfin = _ssm_call(u_p, lag_blocks, to_state, from_state, dec_re, dec_im,
                           w["d_skip"].astype(F32).reshape(nt, 1, LANES), min(2048, s))
    y_p = _post_call(xp, o_p, ys_p, mod_p, *post_w, tm=tm, ff_chunk=d_ff // 2)
    half = hfin.shape[-1] // 2
    hre_p = hfin[:, :, 0, :half].transpose(1, 0, 2).reshape(b, g, p)
    him_p = hfin[:, :, 0, half:].transpose(1, 0, 2).reshape(b, g, p)

    xs2 = xs.reshape(db, d)
    z_s, lf_s, u_s = _pre_sample_call(xs2, mod_s, g_pre, w_qkv, w_f, b_f, w_u)
    q_s = z_s[:, :d_attn].reshape(db, n_heads, HEAD_DIM)
    k_s = z_s[:, d_attn:2 * d_attn].reshape(db, n_heads, HEAD_DIM)
    v_s = z_s[:, 2 * d_attn:].reshape(db, n_heads, HEAD_DIM)
    lfn = lf_s[:, :n_heads]
    n_pages = page_table.shape[1]
    o_s = _dec_call(page_table, q_s, k_s, q_s.transpose(0, 2, 1), v_s.transpose(0, 2, 1),
                    lfn.reshape(db, n_heads, 1), cache_k, cache_v, cache_logf, math.gcd(DEC_PAGES, n_pages))
    o_s = o_s.reshape(db, n_heads, HEAD_DIM)
    o_s = jnp.pad(o_s.transpose(1, 0, 2), ((0, 0), (0, 0), (0, LANES - HEAD_DIM))).astype(BF16)[None]
    ys_s, hre_s, him_s = _ssm_step_call(u_s, h_re, h_im, lam_bar, b_bar, c, w["d_skip"].astype(F32))
    y_s = _post_call(xs2[None], o_s, ys_s[None], mod_s, *post_w, tm=db, ff_chunk=d_ff // 2)

    return dict(
        y_p=y_p, y_s=y_s.reshape(db, 1, d),
        k_p=k_p.transpose(0, 1, 4, 2, 3), v_p=v_p.transpose(0, 1, 4, 2, 3),
        f_p=lf_p.transpose(0, 1, 3, 2), r_p=hre_p, i_p=him_p,
        k_s=k_s.reshape(db, 1, n_heads, HEAD_DIM), v_s=v_s.reshape(db, 1, n_heads, HEAD_DIM),
        f_s=lfn.reshape(db, 1, n_heads), r_s=hre_s.reshape(db, g, p), i_s=him_s.reshape(db, g, p))


def kernel(x_prompt, x_sample, c_prompt, c_sample, cache_k, cache_v, cache_logf, state_ssm_re, state_ssm_im,
           page_table, w_ada, b_ada, g_pre_mix, g_post_mix, g_pre_ffn, g_post_ffn, w_in, b_f, a_re, a_im,
           log_dt, b_re, b_im, c_re, c_im, d_skip, w_glu, b_glu, g_attn_out, g_ssm_out, w_out, w_gate_up, w_down):
    depth = w_in.shape[0]
    assert depth == 1 and x_sample.shape[1] == 1, "single layer, one new token per sequence"
    weights = dict(w_ada=w_ada, b_ada=b_ada, g_pre_mix=g_pre_mix, g_post_mix=g_post_mix, g_pre_ffn=g_pre_ffn,
                   g_post_ffn=g_post_ffn, w_in=w_in, b_f=b_f, a_re=a_re, a_im=a_im, log_dt=log_dt, b_re=b_re,
                   b_im=b_im, c_re=c_re, c_im=c_im, d_skip=d_skip, w_glu=w_glu, b_glu=b_glu,
                   g_attn_out=g_attn_out, g_ssm_out=g_ssm_out, w_out=w_out, w_gate_up=w_gate_up, w_down=w_down)
    w0 = {name: val[0] for name, val in weights.items()}
    r = _layer(x_prompt.astype(F32), x_sample.astype(F32), c_prompt, c_sample, cache_k, cache_v, cache_logf,
               state_ssm_re[0], state_ssm_im[0], page_table, w0)
    stack = lambda a: a[None]
    return (r["y_p"].astype(x_prompt.dtype), r["y_s"].astype(x_sample.dtype),
            stack(r["k_p"]), stack(r["v_p"]), stack(r["f_p"]), stack(r["r_p"]), stack(r["i_p"]),
            stack(r["k_s"]), stack(r["v_s"]), stack(r["f_s"]), stack(r["r_s"]), stack(r["i_s"]))
```

```python
import functools
import math

import numpy as np
import jax
import jax.numpy as jnp
from jax import lax
from jax.experimental import pallas as pl
from jax.experimental.pallas import tpu as pltpu

F32 = jnp.float32
BF16 = jnp.bfloat16
HI = lax.Precision.HIGHEST
EPS = 1e-6
NEG = -1e30
HEAD_DIM = 64
SSM_GROUP = 16
LANES = 128
GROUPS_PER_TILE = LANES // SSM_GROUP
VMEM_LIMIT = 56 * 1024 * 1024
ROW_TILE = 512
SSM_ROWS = 2048
SSM_CHUNK = 8
V_ROWS = 80
SQRT_2_OVER_PI = math.sqrt(2.0 / math.pi)
LOG2_E = math.log2(math.e)


def _sigmoid(x):
    return 1.0 / (1.0 + jnp.exp(-x))


def _silu(x):
    return x * _sigmoid(x)


def _log_sigmoid(x):
    return jnp.minimum(x, 0.0) - jnp.log1p(jnp.exp(-jnp.abs(x)))


def _gelu_tanh(x):
    return x * (0.5 * (1.0 + jnp.tanh(SQRT_2_OVER_PI * (x + 0.044715 * (x * x * x)))))


def _rms(x):
    return x * lax.rsqrt(jnp.mean(x * x, axis=-1, keepdims=True) + EPS)


def _params(*sem):
    return pltpu.CompilerParams(dimension_semantics=sem, vmem_limit_bytes=VMEM_LIMIT)


def _mod_kernel(c_ref, w_ref, b_ref, o_ref):
    c = c_ref[...]
    o_ref[...] = jnp.dot(_silu(c), w_ref[...], precision=HI, preferred_element_type=F32) + b_ref[...]


def _mod_call(c_all, w_ada, b_ada):
    rows, d = c_all.shape
    n = w_ada.shape[1]
    return pl.pallas_call(
        _mod_kernel,
        grid=(n // d,),
        in_specs=[pl.BlockSpec((rows, d), lambda i: (0, 0)),
                  pl.BlockSpec((d, d), lambda i: (0, i)),
                  pl.BlockSpec((1, d), lambda i: (0, i))],
        out_specs=pl.BlockSpec((rows, d), lambda i: (0, i)),
        out_shape=jax.ShapeDtypeStruct((rows, n), F32),
        compiler_params=_params("arbitrary"),
        name="mod",
    )(c_all, w_ada, b_ada)


def _split3(f):
    hi = f.astype(BF16)
    r1 = f - hi.astype(F32)
    mid = r1.astype(BF16)
    lo = (r1 - mid.astype(F32)).astype(BF16)
    return hi, mid, lo


def _pre_prompt_kernel(x_ref, mod_ref, g_ref, wqkv_ref, wfh_ref, wfl_ref, bf_ref, wu_ref, tri_ref, paug_ref,
                       caug_ref, kt_ref, vt_ref, lft_ref, qa_ref, ka_ref, va_ref, u_ref, carry_ref,
                       *, n_heads, d_attn):
    @pl.when(pl.program_id(1) == 0)
    def _():
        carry_ref[...] = jnp.zeros_like(carry_ref)

    x = x_ref[0]
    tm = x.shape[0]
    h = _rms(x) * g_ref[...] * (1.0 + mod_ref[0, 1]) + mod_ref[0, 0]
    hb = h.astype(BF16)
    z = jnp.dot(hb, wqkv_ref[...], preferred_element_type=F32)
    u_ref[0] = jnp.dot(hb, wu_ref[...], preferred_element_type=F32)

    h_lo = (h - hb.astype(F32)).astype(BF16)
    fl = (jnp.dot(hb, wfh_ref[...], preferred_element_type=F32) + jnp.dot(hb, wfl_ref[...], preferred_element_type=F32)
          + jnp.dot(h_lo, wfh_ref[...], preferred_element_type=F32) + bf_ref[...])
    lane = lax.broadcasted_iota(jnp.int32, (tm, LANES), 1)
    logf = jnp.where(lane < n_heads, _log_sigmoid(fl), 0.0)
    logf_t = logf.T
    for pg in range(tm // LANES):
        lft_ref[0, pg] = logf_t[:n_heads, pg * LANES:(pg + 1) * LANES]
    tri_sum = jnp.dot(tri_ref[...], jnp.concatenate(_split3(logf), axis=1), preferred_element_type=F32)
    cum = tri_sum[:, :LANES] + tri_sum[:, LANES:2 * LANES] + tri_sum[:, 2 * LANES:] + carry_ref[...]
    carry_ref[...] = cum[tm - 1:tm, :]

    fs = jnp.concatenate(_split3(cum * LOG2_E), axis=1)
    aug = jnp.dot(fs, paug_ref[...], preferred_element_type=F32) + caug_ref[...]
    augq, augk = aug[:, :LANES], aug[:, LANES:]
    extra = (lane >= HEAD_DIM) & (lane < HEAD_DIM + AUG_LANES)
    low = lane < HEAD_DIM
    vone = jnp.where(lane == HEAD_DIM, 1.0, 0.0)
    scale = HEAD_DIM ** -0.5 * LOG2_E
    for j in range(n_heads // 2):
        zq = z[:, j * LANES:(j + 1) * LANES] * scale
        zk = z[:, d_attn + j * LANES:d_attn + (j + 1) * LANES]
        zv = z[:, 2 * d_attn + j * LANES:2 * d_attn + (j + 1) * LANES]
        for par in range(2):
            hh = 2 * j + par
            if par:
                zq, zk, zv = (pltpu.roll(a, HEAD_DIM, 1) for a in (zq, zk, zv))
            shift = (HEAD_DIM - AUG_STRIDE * hh) % LANES
            aq, ak = ((pltpu.roll(a, shift, 1) if shift else a) for a in (augq, augk))
            qa_ref[0, hh] = jnp.where(low, zq, jnp.where(extra, aq, 0.0)).astype(BF16)
            ka_ref[0, hh] = jnp.where(low, zk, jnp.where(extra, ak, 0.0)).astype(BF16)
            k_t = zk.T
            v_t = jnp.where(low, zv, vone).T
            va_ref[0, hh, 0] = v_t[:V_ROWS].astype(BF16)
            for pg in range(tm // LANES):
                kt_ref[0, pg, hh] = k_t[:HEAD_DIM, pg * LANES:(pg + 1) * LANES]
                vt_ref[0, pg, hh] = v_t[:HEAD_DIM, pg * LANES:(pg + 1) * LANES]


AUG_LANES = 6
AUG_STRIDE = 16


def _aug_constants(n_heads):
    assert n_heads * AUG_STRIDE <= LANES and AUG_LANES <= AUG_STRIDE
    place = np.zeros((3 * LANES, 2 * LANES), np.float32)
    const = np.zeros((1, 2 * LANES), np.float32)
    for h in range(n_heads):
        for piece in range(3):
            place[piece * LANES + h, h * AUG_STRIDE + piece] = 1.0
            const[0, h * AUG_STRIDE + 3 + piece] = 1.0
            const[0, LANES + h * AUG_STRIDE + piece] = 1.0
            place[piece * LANES + h, LANES + h * AUG_STRIDE + 3 + piece] = -1.0
    return jnp.asarray(place, BF16), jnp.asarray(const)


def _pre_prompt_call(x, mod_p, g_pre, w_qkv, w_f, b_f, w_u, n_heads, tm):
    b, s, d = x.shape
    d_attn = n_heads * HEAD_DIM
    d_ssm = w_u.shape[1]
    tri = jnp.asarray(np.tril(np.ones((tm, tm), np.float32)), BF16)
    w_f_hi = w_f.astype(BF16)
    w_f_lo = (w_f - w_f_hi.astype(F32)).astype(BF16)
    place, place_const = _aug_constants(n_heads)
    pages = tm // LANES
    const = lambda *shape: pl.BlockSpec(shape, lambda bi, ti: (0,) * len(shape))
    rows = lambda width: pl.BlockSpec((1, tm, width), lambda bi, ti: (bi, ti, 0))
    heads = pl.BlockSpec((1, n_heads, tm, LANES), lambda bi, ti: (bi, 0, ti, 0))
    heads_t = pl.BlockSpec((1, n_heads, 1, V_ROWS, tm), lambda bi, ti: (bi, 0, ti, 0, 0))
    paged = pl.BlockSpec((1, pages, n_heads, HEAD_DIM, LANES), lambda bi, ti: (bi, ti, 0, 0, 0))
    paged_shape = jax.ShapeDtypeStruct((b, s // LANES, n_heads, HEAD_DIM, LANES), F32)
    aug_shape = jax.ShapeDtypeStruct((b, n_heads, s, LANES), BF16)
    aug_t_shape = jax.ShapeDtypeStruct((b, n_heads, s // tm, V_ROWS, tm), BF16)
    return pl.pallas_call(
        functools.partial(_pre_prompt_kernel, n_heads=n_heads, d_attn=d_attn),
        grid=(b, s // tm),
        in_specs=[rows(d),
                  pl.BlockSpec((1, 6, 1, d), lambda bi, ti: (bi, 0, 0, 0)),
                  const(1, d), const(d, 3 * d_attn), const(d, LANES), const(d, LANES), const(1, LANES),
                  const(d, d_ssm), const(tm, tm), const(3 * LANES, 2 * LANES), const(1, 2 * LANES)],
        out_specs=[paged, paged, pl.BlockSpec((1, pages, n_heads, LANES), lambda bi, ti: (bi, ti, 0, 0)),
                   heads, heads, heads_t, rows(d_ssm)],
        out_shape=[paged_shape, paged_shape, jax.ShapeDtypeStruct((b, s // LANES, n_heads, LANES), F32),
                   aug_shape, aug_shape, aug_t_shape, jax.ShapeDtypeStruct((b, s, d_ssm), F32)],
        scratch_shapes=[pltpu.VMEM((1, LANES), F32)],
        compiler_params=_params("arbitrary", "arbitrary"),
        name="pre_prompt",
    )(x, mod_p, g_pre, w_qkv, w_f_hi, w_f_lo, b_f, w_u, tri, place, place_const)


def _pre_sample_kernel(x_ref, mod_ref, g_ref, wqkv_ref, wf_ref, bf_ref, wu_ref, z_ref, lf_ref, u_ref):
    x = x_ref[...]
    h = _rms(x) * g_ref[...] * (1.0 + mod_ref[0, 1]) + mod_ref[0, 0]
    hb = h.astype(BF16)
    z_ref[...] = jnp.dot(hb, wqkv_ref[...], preferred_element_type=F32)
    u_ref[...] = jnp.dot(hb, wu_ref[...], preferred_element_type=F32)
    fl = jnp.dot(h, wf_ref[...], precision=HI, preferred_element_type=F32) + bf_ref[...]
    lf_ref[...] = _log_sigmoid(fl)


def _pre_sample_call(x, mod_s, g_pre, w_qkv, w_f, b_f, w_u):
    rows = x.shape[0]
    return pl.pallas_call(
        _pre_sample_kernel,
        out_shape=[jax.ShapeDtypeStruct((rows, w_qkv.shape[1]), F32),
                   jax.ShapeDtypeStruct((rows, LANES), F32),
                   jax.ShapeDtypeStruct((rows, w_u.shape[1]), F32)],
        compiler_params=pltpu.CompilerParams(vmem_limit_bytes=VMEM_LIMIT),
        name="pre_sample",
    )(x, mod_s, g_pre, w_qkv, w_f, b_f, w_u)


FLASH_HEADS = 4


def _flash_kernel(q_ref, k_ref, vt_ref, o_ref, st_ref, bmax_ref, m_ref, acc_ref, *, tq):
    qi = pl.program_id(2)
    heads = q_ref.shape[1]

    def put_scores(kb, buf):
        off = pl.multiple_of(kb * tq, tq)
        for h in range(heads):
            st = lax.dot_general(k_ref[0, h, pl.ds(off, tq), :], q_ref[0, h], (((1,), (1,)), ((), ())),
                                 preferred_element_type=F32)
            st_ref[h, buf] = st
            bmax_ref[h, buf] = jnp.max(st, axis=0, keepdims=True)

    def update(kb, buf, diagonal):
        for h in range(heads):
            st = st_ref[h, buf]
            if diagonal:
                key = lax.broadcasted_iota(jnp.int32, st.shape, 0)
                qry = lax.broadcasted_iota(jnp.int32, st.shape, 1)
                st = jnp.where(key <= qry, st, NEG)
                block_max = jnp.max(st, axis=0, keepdims=True)
            else:
                block_max = bmax_ref[h, buf]
            m = m_ref[h]
            m_new = jnp.maximum(m, block_max)
            acc_ref[h] = acc_ref[h] * jnp.exp2(m - m_new) + jnp.dot(
                vt_ref[0, h, kb], jnp.exp2(st - m_new).astype(BF16), preferred_element_type=F32)
            m_ref[h] = m_new

    def finish():
        for h in range(heads):
            acc = acc_ref[h]
            o = acc[:HEAD_DIM] * (1.0 / acc[HEAD_DIM:HEAD_DIM + 1, :])
            o_ref[0, h] = jnp.concatenate([o, jnp.zeros((LANES - HEAD_DIM, tq), F32)], axis=0).T.astype(BF16)

    m_ref[...] = jnp.full_like(m_ref, NEG)
    acc_ref[...] = jnp.zeros_like(acc_ref)
    put_scores(0, 0)

    def pair(j, carry):
        put_scores(2 * j + 1, 1)
        update(2 * j, 0, False)
        put_scores(2 * j + 2, 0)
        update(2 * j + 1, 1, False)
        return carry

    lax.fori_loop(0, qi // 2, pair, 0)

    @pl.when(qi % 2 == 1)
    def _():
        put_scores(qi, 1)
        update(qi - 1, 0, False)
        update(qi, 1, True)
        finish()

    @pl.when(qi % 2 == 0)
    def _():
        update(qi, 0, True)
        finish()


def _flash_call(qa, ka, vat, tq):
    b, h, s, _ = qa.shape
    assert vat.shape[-1] == tq, "value chunks are laid out per key block"
    nh = math.gcd(FLASH_HEADS, h)
    qspec = pl.BlockSpec((1, nh, tq, LANES), lambda bi, hi, qi: (bi, hi, qi, 0))
    once = pl.Buffered(1)
    kspec = pl.BlockSpec((1, nh, s, LANES), lambda bi, hi, qi: (bi, hi, 0, 0), pipeline_mode=once)
    vspec = pl.BlockSpec((1, nh, s // tq, V_ROWS, tq), lambda bi, hi, qi: (bi, hi, 0, 0, 0), pipeline_mode=once)
    return pl.pallas_call(
        functools.partial(_flash_kernel, tq=tq),
        grid=(b, h // nh, s // tq),
        in_specs=[qspec, kspec, vspec],
        out_specs=qspec,
        out_shape=jax.ShapeDtypeStruct((b, h, s, LANES), BF16),
        scratch_shapes=[pltpu.VMEM((nh, 2, tq, tq), F32), pltpu.VMEM((nh, 2, 1, tq), F32),
                        pltpu.VMEM((nh, 1, tq), F32),
                        pltpu.VMEM((nh, V_ROWS, tq), F32)],
        compiler_params=_params("arbitrary", "arbitrary", "arbitrary"),
        name="flash",
    )(qa, ka, vat)


def _cmul(ar, ai, br, bi):
    return ar * br - ai * bi, ar * bi + ai * br


def _ssm_discretize(a_re, a_im, log_dt, b_re, b_im, c_re, c_im):
    ar, ai = a_re.astype(F32), a_im.astype(F32)
    dt = jnp.exp(log_dt.astype(F32))[:, None]
    lam_dt = (ar * dt, ai * dt)
    mag = jnp.exp(lam_dt[0])
    lam_bar = (mag * jnp.cos(lam_dt[1]), mag * jnp.sin(lam_dt[1]))
    den = ar * ar + ai * ai
    nr, ni = lam_bar[0] - 1.0, lam_bar[1]
    coef = ((nr * ar + ni * ai) / den, (ni * ar - nr * ai) / den)
    b_bar = _cmul(coef[0][..., None], coef[1][..., None], b_re.astype(F32), b_im.astype(F32))
    c = (c_re.astype(F32), c_im.astype(F32))
    return lam_dt, lam_bar, b_bar, c


def _ssm_chunk_operators(lam_dt, b_bar, c, chunk):
    g, p = lam_dt[0].shape
    nt = g // GROUPS_PER_TILE
    steps = jnp.arange(chunk + 1, dtype=F32)[:, None, None]
    mag = jnp.exp(steps * lam_dt[0][None])
    pw = (mag * jnp.cos(steps * lam_dt[1][None]), mag * jnp.sin(steps * lam_dt[1][None]))
    gpt = GROUPS_PER_TILE
    group_of = lambda cols, width: jnp.asarray(
        (np.arange(cols)[None, :] // width) % gpt == np.arange(gpt)[:, None], F32)
    pb = _cmul(pw[0][:chunk, :, :, None], pw[1][:chunk, :, :, None], b_bar[0][None], b_bar[1][None])
    kmat = (jnp.einsum("ghp,dgpk->dkgh", c[0], pb[0], precision=HI)
            - jnp.einsum("ghp,dgpk->dkgh", c[1], pb[1], precision=HI))
    kc = kmat.reshape(chunk, SSM_GROUP, nt, LANES).transpose(2, 0, 1, 3)
    lag_blocks = (kc[:, :, None] * group_of(LANES, SSM_GROUP)[None, None, :, None, :]
                  ).reshape(nt, chunk, LANES, LANES).astype(BF16)
    rev = chunk - 1 - np.arange(chunk)
    wb = _cmul(pw[0][rev][..., None], pw[1][rev][..., None], b_bar[0][None], b_bar[1][None])
    wsm = jnp.stack([part.reshape(chunk, nt, gpt, p, SSM_GROUP) for part in wb])
    wsm = wsm.transpose(2, 1, 5, 0, 3, 4).reshape(nt, chunk, SSM_GROUP, 2 * gpt * p)
    to_state = (wsm[:, :, None] * group_of(2 * gpt * p, p)[None, None, :, None, :]
                ).reshape(nt, chunk * LANES, 2 * gpt * p).astype(BF16)
    cp = _cmul(c[0][None], c[1][None], pw[0][1:chunk + 1][:, :, None, :], pw[1][1:chunk + 1][:, :, None, :])
    csm = jnp.stack([part.reshape(chunk, nt, gpt, SSM_GROUP, p) for part in (cp[0], -cp[1])])
    csm = csm.transpose(2, 0, 5, 1, 3, 4).reshape(nt, 2, p, chunk * LANES)
    from_state = (csm[:, :, None] * group_of(chunk * LANES, SSM_GROUP)[None, None, :, None, :]
                  ).reshape(nt, 2 * gpt * p, chunk * LANES).astype(BF16)
    dec_re = pw[0][chunk].reshape(nt, 1, gpt * p)
    dec_im = pw[1][chunk].reshape(nt, 1, gpt * p)
    return lag_blocks, to_state, from_state, dec_re, dec_im


def _ssm_kernel(u_ref, lag_ref, g_ref, c_ref, are_ref, aim_ref, d_ref, y_ref, hfin_ref, xin_ref, xs_ref, st_ref,
                toep_ref, *, chunk, n_chunks):
    r = pl.program_id(2)

    @pl.when((pl.program_id(1) == 0) & (r == 0))
    def _():
        for l_in in range(chunk):
            for l_out in range(chunk):
                blk = lag_ref[0, l_out - l_in] if l_out >= l_in else jnp.zeros((LANES, LANES), BF16)
                toep_ref[l_in * LANES:(l_in + 1) * LANES, l_out * LANES:(l_out + 1) * LANES] = blk

    @pl.when(r == 0)
    def _():
        st_ref[...] = jnp.zeros_like(st_ref)

    half = st_ref.shape[1] // 2
    pieces = [u_ref[0, pl.ds(l, n_chunks, stride=chunk), :] for l in range(chunk)]
    u2 = jnp.concatenate([pc.astype(BF16) for pc in pieces], axis=1)
    xin_ref[...] = jnp.dot(u2, g_ref[0], preferred_element_type=F32)
    ar = are_ref[0]
    ai = aim_ref[0]

    def body(i, carry):
        xr, xi = carry
        base = pl.multiple_of(i * 8, 8)
        blk = xin_ref[pl.ds(base, 8), :]
        rows_r, rows_i = [], []
        for rr in range(8):
            rows_r.append(xr)
            rows_i.append(xi)
            xr, xi = (ar * xr - ai * xi + blk[rr:rr + 1, :half],
                      ar * xi + ai * xr + blk[rr:rr + 1, half:])
        xs_ref[pl.ds(base, 8), :] = jnp.concatenate(
            [jnp.concatenate(rows_r, axis=0), jnp.concatenate(rows_i, axis=0)], axis=1)
        return xr, xi

    xr, xi = lax.fori_loop(0, n_chunks // 8, body, (st_ref[:, :half], st_ref[:, half:]), unroll=True)
    st_ref[...] = jnp.concatenate([xr, xi], axis=1)
    y2 = (jnp.dot(u2, toep_ref[...], preferred_element_type=F32)
          + jnp.dot(xs_ref[...].astype(BF16), c_ref[0], preferred_element_type=F32))
    dd = d_ref[0]
    for l in range(chunk):
        y_ref[0, pl.ds(l, n_chunks, stride=chunk), :] = y2[:, l * LANES:(l + 1) * LANES] + dd * pieces[l]

    @pl.when(r == pl.num_programs(2) - 1)
    def _():
        hfin_ref[0, 0] = st_ref[...]


def _ssm_call(u, lag_blocks, to_state, from_state, dec_re, dec_im, d_skip, rows_per_step):
    b, s, d_ssm = u.shape
    nt = d_ssm // LANES
    chunk = lag_blocks.shape[1]
    n_chunks = rows_per_step // chunk
    nstate = to_state.shape[2]
    tile = lambda *shape: pl.BlockSpec((1,) + shape, lambda j, bi, r: (j, 0, 0))
    useq = pl.BlockSpec((1, rows_per_step, LANES), lambda j, bi, r: (bi, r, j))
    return pl.pallas_call(
        functools.partial(_ssm_kernel, chunk=chunk, n_chunks=n_chunks),
        grid=(nt, b, s // rows_per_step),
        in_specs=[useq, pl.BlockSpec((1, chunk, LANES, LANES), lambda j, bi, r: (j, 0, 0, 0)),
                  tile(chunk * LANES, nstate),
                  tile(nstate, chunk * LANES), tile(1, nstate // 2), tile(1, nstate // 2), tile(1, LANES)],
        out_specs=[useq, pl.BlockSpec((1, 1, 1, nstate), lambda j, bi, r: (j, bi, 0, 0))],
        out_shape=[jax.ShapeDtypeStruct((b, s, d_ssm), F32), jax.ShapeDtypeStruct((nt, b, 1, nstate), F32)],
        scratch_shapes=[pltpu.VMEM((n_chunks, nstate), F32), pltpu.VMEM((n_chunks, nstate), F32),
                        pltpu.VMEM((1, nstate), F32), pltpu.VMEM((chunk * LANES, chunk * LANES), BF16)],
        compiler_params=_params("arbitrary", "arbitrary", "arbitrary"),
        name="ssm",
    )(u, lag_blocks, to_state, from_state, dec_re, dec_im, d_skip)


def _ssm_step_kernel(u_ref, hre_ref, him_ref, bre_ref, bim_ref, lre_ref, lim_ref, cre_ref, cim_ref, d_ref,
                     y_ref, xre_ref, xim_ref):
    u = u_ref[...]
    hr, hi = hre_ref[...], him_ref[...]
    lr, li = lre_ref[...], lim_ref[...]
    xr = lr * hr - li * hi + jnp.dot(u, bre_ref[...], precision=HI, preferred_element_type=F32)
    xi = lr * hi + li * hr + jnp.dot(u, bim_ref[...], precision=HI, preferred_element_type=F32)
    xre_ref[...] = xr
    xim_ref[...] = xi
    y_ref[...] = (jnp.dot(xr, cre_ref[...], precision=HI, preferred_element_type=F32)
                  - jnp.dot(xi, cim_ref[...], precision=HI, preferred_element_type=F32) + d_ref[...] * u)


def _ssm_step_call(u, h_re, h_im, lam_bar, b_bar, c, d_skip):
    rows, d_ssm = u.shape
    g, p = lam_bar[0].shape
    eye = jnp.eye(g, dtype=F32)
    bmat = [jnp.einsum("gpk,gf->gkfp", part, eye, precision=HI).reshape(d_ssm, g * p) for part in b_bar]
    cmat = [jnp.einsum("ghp,gf->gpfh", part, eye, precision=HI).reshape(g * p, d_ssm) for part in c]
    return pl.pallas_call(
        _ssm_step_kernel,
        out_shape=[jax.ShapeDtypeStruct((rows, d_ssm), F32), jax.ShapeDtypeStruct((rows, g * p), F32),
                   jax.ShapeDtypeStruct((rows, g * p), F32)],
        compiler_params=pltpu.CompilerParams(vmem_limit_bytes=VMEM_LIMIT),
        name="ssm_step",
    )(u, h_re.reshape(rows, g * p), h_im.reshape(rows, g * p), bmat[0], bmat[1],
      lam_bar[0].reshape(1, g * p), lam_bar[1].reshape(1, g * p), cmat[0], cmat[1], d_skip.reshape(1, d_ssm))


DEC_PAGES = 16


def _dec_kernel(pt_ref, q_ref, kn_ref, qt_ref, vnt_ref, lfn_ref, sli_ref, *rest, n_heads, pages):
    del pt_ref
    kt_refs, vt_refs, lf_refs = rest[:pages], rest[pages:2 * pages], rest[2 * pages:3 * pages]
    o_ref, qb_ref, m_ref, l_ref, acc_ref, carry_ref = rest[3 * pages:]
    step = pl.program_id(1)
    scale = HEAD_DIM ** -0.5

    @pl.when(step == 0)
    def _():
        qt = qt_ref[0] * scale
        vnt = vnt_ref[0]
        lane = lax.broadcasted_iota(jnp.int32, (HEAD_DIM, LANES), 1)
        for h in range(n_heads):
            qb_ref[h] = jnp.broadcast_to(qt[:, h:h + 1], (HEAD_DIM, LANES))
            acc_ref[h] = jnp.where(lane == 0, jnp.broadcast_to(vnt[:, h:h + 1], (HEAD_DIM, LANES)), 0.0)
        m_ref[...] = jnp.sum(q_ref[0] * kn_ref[0], axis=1, keepdims=True) * scale
        l_ref[...] = jnp.ones_like(l_ref)
        carry_ref[...] = lfn_ref[0]

    lf_all = jnp.concatenate([r[0, 0] for r in lf_refs], axis=0)
    sli = sli_ref[...]
    incl = sum(jnp.dot(piece, sli, preferred_element_type=F32) for piece in _split3(lf_all))
    run = carry_ref[...]
    scores = [None] * pages
    for i in reversed(range(pages)):
        rows = slice(i * n_heads, (i + 1) * n_heads)
        bias = incl[rows] - lf_all[rows] + run
        run = run + incl[rows][:, 0:1]
        qk = [jnp.sum(kt_refs[i][0, 0, h] * qb_ref[h], axis=0, keepdims=True) for h in range(n_heads)]
        scores[i] = jnp.concatenate(qk, axis=0) + bias
    carry_ref[...] = run

    m_old = m_ref[...]
    m_new = jnp.maximum(m_old, jnp.max(functools.reduce(jnp.maximum, scores), axis=1, keepdims=True))
    corr = jnp.exp(m_old - m_new)
    probs = [jnp.exp(s - m_new) for s in scores]
    l_ref[...] = l_ref[...] * corr + jnp.sum(functools.reduce(jnp.add, probs), axis=1, keepdims=True)
    m_ref[...] = m_new
    for h in range(n_heads):
        a = acc_ref[h] * corr[h:h + 1, :]
        for i in range(pages):
            a = a + vt_refs[i][0, 0, h] * probs[i][h:h + 1, :]
        acc_ref[h] = a

    @pl.when(step == pl.num_programs(1) - 1)
    def _():
        inv = 1.0 / l_ref[...]
        ones = jnp.ones((8, LANES), BF16)
        for h in range(n_heads):
            sums = sum(lax.dot_general(ones, piece, (((1,), (1,)), ((), ())), preferred_element_type=F32)
                       for piece in _split3(acc_ref[h]))
            o_ref[0, h:h + 1, :] = sums[0:1, :] * inv[h:h + 1, :]


def _dec_call(page_table, q, kn, qt, vnt, lfn, cache_k, cache_v, cache_logf, pages):
    db, n_pages = page_table.shape
    _, _, page, n_heads, _ = cache_k.shape
    assert page == LANES and n_pages % pages == 0
    kt = jnp.transpose(cache_k, (0, 1, 3, 4, 2))
    vt = jnp.transpose(cache_v, (0, 1, 3, 4, 2))
    lft = jnp.transpose(cache_logf, (0, 1, 3, 2))
    sli = jnp.asarray(np.tril(np.ones((page, page), np.float32)), BF16)
    pt_flat = page_table.reshape(-1)

    def phys(i):
        return lambda bi, st, pt: pt[bi * n_pages + n_pages - pages * (st + 1) + i]

    per_b = lambda *shape: pl.BlockSpec((1,) + shape, lambda bi, st, pt: (bi,) + (0,) * len(shape))
    kv_spec = lambda i: pl.BlockSpec((1, 1, n_heads, HEAD_DIM, page),
                                     lambda bi, st, pt, f=phys(i): (0, f(bi, st, pt), 0, 0, 0))
    lf_spec = lambda i: pl.BlockSpec((1, 1, n_heads, page), lambda bi, st, pt, f=phys(i): (0, f(bi, st, pt), 0, 0))
    grid_spec = pltpu.PrefetchScalarGridSpec(
        num_scalar_prefetch=1,
        grid=(db, n_pages // pages),
        in_specs=([per_b(n_heads, HEAD_DIM), per_b(n_heads, HEAD_DIM), per_b(HEAD_DIM, n_heads),
                   per_b(HEAD_DIM, n_heads), per_b(n_heads, 1), pl.BlockSpec((page, page), lambda bi, st, pt: (0, 0))]
                  + [kv_spec(i) for i in range(pages)] + [kv_spec(i) for i in range(pages)]
                  + [lf_spec(i) for i in range(pages)]),
        out_specs=per_b(n_heads, HEAD_DIM),
        scratch_shapes=[pltpu.VMEM((n_heads, HEAD_DIM, page), F32), pltpu.VMEM((n_heads, 1), F32),
                        pltpu.VMEM((n_heads, 1), F32), pltpu.VMEM((n_heads, HEAD_DIM, page), F32),
                        pltpu.VMEM((n_heads, 1), F32)])
    return pl.pallas_call(
        functools.partial(_dec_kernel, n_heads=n_heads, pages=pages),
        grid_spec=grid_spec,
        out_shape=jax.ShapeDtypeStruct((db, n_heads, HEAD_DIM), F32),
        compiler_params=_params("arbitrary", "arbitrary"),
        name="dec",
    )(pt_flat, q, kn, qt, vnt, lfn, sli, *([kt] * pages), *([vt] * pages), *([lft] * pages))


def _post_kernel(x_ref, o_ref, ys_ref, mod_ref, wglu_ref, bglu_ref, gatt_ref, gssm_ref, wout_ref, gpm_ref,
                 gpf_ref, wgu_ref, wdn_ref, gpo_ref, y_ref, *, n_heads, ff_chunk):
    x = x_ref[0]
    pairs = []
    for j in range(n_heads // 2):
        even = o_ref[0, 2 * j].astype(F32)
        odd = o_ref[0, 2 * j + 1].astype(F32)
        pairs.append(even + pltpu.roll(odd, HEAD_DIM, 1))
    attn = jnp.concatenate(pairs, axis=1)
    gl = _gelu_tanh(ys_ref[0])
    gl = gl * _sigmoid(jnp.dot(gl.astype(BF16), wglu_ref[...], preferred_element_type=F32) + bglu_ref[...])
    mix = jnp.concatenate([_rms(attn) * gatt_ref[...], _rms(gl) * gssm_ref[...]], axis=1)
    mo = jnp.dot(mix.astype(BF16), wout_ref[...], preferred_element_type=F32)
    x1 = x + mod_ref[0, 2] * (_rms(mo) * gpm_ref[...])
    h2 = (_rms(x1) * gpf_ref[...] * (1.0 + mod_ref[0, 4]) + mod_ref[0, 3]).astype(BF16)
    d_ff = wdn_ref.shape[0]
    acc = jnp.zeros_like(x)
    for c0 in range(0, d_ff, ff_chunk):
        gate = jnp.dot(h2, wgu_ref[:, c0:c0 + ff_chunk], preferred_element_type=F32)
        up = jnp.dot(h2, wgu_ref[:, d_ff + c0:d_ff + c0 + ff_chunk], preferred_element_type=F32)
        acc = acc + jnp.dot((_silu(gate) * up).astype(BF16), wdn_ref[c0:c0 + ff_chunk, :],
                            preferred_element_type=F32)
    y_ref[0] = x1 + mod_ref[0, 5] * (_rms(acc) * gpo_ref[...])


def _post_call(x, o, ys, mod, w_glu, b_glu, g_attn, g_ssm, w_out, g_post_mix, g_pre_ffn, w_gate_up, w_down,
               g_post_ffn, tm, ff_chunk):
    nb, s, d = x.shape
    n_heads = o.shape[1]
    d_ssm = ys.shape[2]
    mod_rows = mod.shape[2]
    const = lambda a: pl.BlockSpec(a.shape, lambda bi, ti: (0,) * a.ndim, pipeline_mode=pl.Buffered(1))
    rows = lambda width: pl.BlockSpec((1, tm, width), lambda bi, ti: (bi, ti, 0))
    mod_spec = (pl.BlockSpec((1, 6, 1, d), lambda bi, ti: (bi, 0, 0, 0)) if mod_rows == 1
                else pl.BlockSpec((1, 6, tm, d), lambda bi, ti: (bi, 0, ti, 0)))
    weights = (w_glu, b_glu, g_attn, g_ssm, w_out, g_post_mix, g_pre_ffn, w_gate_up, w_down, g_post_ffn)
    return pl.pallas_call(
        functools.partial(_post_kernel, n_heads=n_heads, ff_chunk=ff_chunk),
        grid=(nb, s // tm),
        in_specs=[rows(d), pl.BlockSpec((1, n_heads, tm, LANES), lambda bi, ti: (bi, 0, ti, 0)), rows(d_ssm),
                  mod_spec] + [const(w) for w in weights],
        out_specs=rows(d),
        out_shape=jax.ShapeDtypeStruct((nb, s, d), F32),
        compiler_params=_params("arbitrary", "arbitrary"),
        name="post",
    )(x, o, ys, mod, *weights)


def _row(v):
    return v.reshape(1, -1).astype(F32)


def _layer(xp, xs, cp, cs, cache_k, cache_v, cache_logf, h_re, h_im, page_table, w):
    b, s, d = xp.shape
    db = xs.shape[0]
    n_heads = w["b_f"].shape[0]
    d_attn = n_heads * HEAD_DIM
    d_ff = w["w_down"].shape[0]
    g, p = w["a_re"].shape
    d_ssm = g * SSM_GROUP

    w_in = w["w_in"]
    w_qkv = w_in[:, :3 * d_attn].astype(BF16)
    w_f = jnp.pad(w_in[:, 3 * d_attn:3 * d_attn + n_heads].astype(F32), ((0, 0), (0, LANES - n_heads)))
    b_f = jnp.pad(_row(w["b_f"]), ((0, 0), (0, LANES - n_heads)))
    w_u = w_in[:, 3 * d_attn + n_heads:].astype(BF16)
    post_w = (w["w_glu"].astype(BF16), _row(w["b_glu"]), _row(w["g_attn_out"]), _row(w["g_ssm_out"]),
              w["w_out"].astype(BF16), _row(w["g_post_mix"]), _row(w["g_pre_ffn"]), w["w_gate_up"].astype(BF16),
              w["w_down"].astype(BF16), _row(w["g_post_ffn"]))
    g_pre = _row(w["g_pre_mix"])

    n_cond = b + db
    pad_rows = -n_cond % 8
    c_all = jnp.pad(jnp.concatenate([cp, cs], axis=0).astype(F32), ((0, pad_rows), (0, 0)))
    mod = _mod_call(c_all, w["w_ada"].astype(F32), _row(w["b_ada"]))
    mod_p = mod[:b].reshape(b, 6, 1, d)
    mod_s = mod[b:n_cond].reshape(db, 6, d).transpose(1, 0, 2)[None]

    tm = min(ROW_TILE, s)
    assert s % tm == 0 and tm % LANES == 0 and cache_k.shape[2] == LANES
    k_p, v_p, lf_p, qa, ka, va, u_p = _pre_prompt_call(xp, mod_p, g_pre, w_qkv, w_f, b_f, w_u, n_heads, tm)
    o_p = _flash_call(qa, ka, va, tm)
    lam_dt, lam_bar, b_bar, c = _ssm_discretize(w["a_re"], w["a_im"], w["log_dt"], w["b_re"], w["b_im"],
                                                w["c_re"], w["c_im"])
    lag_blocks, to_state, from_state, dec_re, dec_im = _ssm_chunk_operators(lam_dt, b_bar, c, SSM_CHUNK)
    nt = d_ssm // LANES
    ys_p, hfin = _ssm_call(u_p, lag_blocks, to_state, from_state, dec_re, dec_im,
                           w["d_skip"].astype(F32).reshape(nt, 1, LANES), math.gcd(SSM_ROWS, s))
    y_p = _post_call(xp, o_p, ys_p, mod_p, *post_w, tm=tm, ff_chunk=d_ff // 2)
    half = hfin.shape[-1] // 2
    hre_p = hfin[:, :, 0, :half].transpose(1, 0, 2).reshape(b, g, p)
    him_p = hfin[:, :, 0, half:].transpose(1, 0, 2).reshape(b, g, p)

    xs2 = xs.reshape(db, d)
    z_s, lf_s, u_s = _pre_sample_call(xs2, mod_s, g_pre, w_qkv, w_f, b_f, w_u)
    q_s = z_s[:, :d_attn].reshape(db, n_heads, HEAD_DIM)
    k_s = z_s[:, d_attn:2 * d_attn].reshape(db, n_heads, HEAD_DIM)
    v_s = z_s[:, 2 * d_attn:].reshape(db, n_heads, HEAD_DIM)
    lfn = lf_s[:, :n_heads]
    n_pages = page_table.shape[1]
    o_s = _dec_call(page_table, q_s, k_s, q_s.transpose(0, 2, 1), v_s.transpose(0, 2, 1),
                    lfn.reshape(db, n_heads, 1), cache_k, cache_v, cache_logf, math.gcd(DEC_PAGES, n_pages))
    o_s = jnp.pad(o_s.transpose(1, 0, 2), ((0, 0), (0, 0), (0, LANES - HEAD_DIM))).astype(BF16)[None]
    ys_s, hre_s, him_s = _ssm_step_call(u_s, h_re, h_im, lam_bar, b_bar, c, w["d_skip"].astype(F32))
    y_s = _post_call(xs2[None], o_s, ys_s[None], mod_s, *post_w, tm=db, ff_chunk=d_ff // 2)

    return dict(
        y_p=y_p, y_s=y_s.reshape(db, 1, d),
        k_p=k_p.transpose(0, 1, 4, 2, 3), v_p=v_p.transpose(0, 1, 4, 2, 3),
        f_p=lf_p.transpose(0, 1, 3, 2), r_p=hre_p, i_p=him_p,
        k_s=k_s.reshape(db, 1, n_heads, HEAD_DIM), v_s=v_s.reshape(db, 1, n_heads, HEAD_DIM),
        f_s=lfn.reshape(db, 1, n_heads), r_s=hre_s.reshape(db, g, p), i_s=him_s.reshape(db, g, p))


def kernel(x_prompt, x_sample, c_prompt, c_sample, cache_k, cache_v, cache_logf, state_ssm_re, state_ssm_im,
           page_table, w_ada, b_ada, g_pre_mix, g_post_mix, g_pre_ffn, g_post_ffn, w_in, b_f, a_re, a_im,
           log_dt, b_re, b_im, c_re, c_im, d_skip, w_glu, b_glu, g_attn_out, g_ssm_out, w_out, w_gate_up, w_down):
    depth = w_in.shape[0]
    assert depth == 1 and x_sample.shape[1] == 1, "single layer, one new token per sequence"
    weights = dict(w_ada=w_ada, b_ada=b_ada, g_pre_mix=g_pre_mix, g_post_mix=g_post_mix, g_pre_ffn=g_pre_ffn,
                   g_post_ffn=g_post_ffn, w_in=w_in, b_f=b_f, a_re=a_re, a_im=a_im, log_dt=log_dt, b_re=b_re,
                   b_im=b_im, c_re=c_re, c_im=c_im, d_skip=d_skip, w_glu=w_glu, b_glu=b_glu,
                   g_attn_out=g_attn_out, g_ssm_out=g_ssm_out, w_out=w_out, w_gate_up=w_gate_up, w_down=w_down)
    w0 = {name: val[0] for name, val in weights.items()}
    r = _layer(x_prompt.astype(F32), x_sample.astype(F32), c_prompt, c_sample, cache_k, cache_v, cache_logf,
               state_ssm_re[0], state_ssm_im[0], page_table, w0)
    stack = lambda a: a[None]
    return (r["y_p"].astype(x_prompt.dtype), r["y_s"].astype(x_sample.dtype),
            stack(r["k_p"]), stack(r["v_p"]), stack(r["f_p"]), stack(r["r_p"]), stack(r["i_p"]),
            stack(r["k_s"]), stack(r["v_s"]), stack(r["f_s"]), stack(r["r_s"]), stack(r["i_s"]))
```

```python
import functools
import math

import numpy as np
import jax
import jax.numpy as jnp
from jax import lax
from jax.experimental import pallas as pl
from jax.experimental.pallas import tpu as pltpu

F32 = jnp.float32
BF16 = jnp.bfloat16
HI = lax.Precision.HIGHEST
EPS = 1e-6
NEG = -1e30
HEAD_DIM = 64
SSM_GROUP = 16
LANES = 128
GROUPS_PER_TILE = LANES // SSM_GROUP
VMEM_LIMIT = 56 * 1024 * 1024
ROW_TILE = 512
SSM_ROWS = 2048
FFN_CHUNK = 256
SSM_CHUNK = 8
V_ROWS = 80
SQRT_2_OVER_PI = math.sqrt(2.0 / math.pi)
LOG2_E = math.log2(math.e)


def _sigmoid(x):
    return 1.0 / (1.0 + jnp.exp(-x))


def _silu(x):
    return x * _sigmoid(x)


def _log_sigmoid(x):
    return jnp.minimum(x, 0.0) - jnp.log1p(jnp.exp(-jnp.abs(x)))


def _gelu_tanh(x):
    return x * (0.5 * (1.0 + jnp.tanh(SQRT_2_OVER_PI * (x + 0.044715 * (x * x * x)))))


def _rms(x):
    return x * lax.rsqrt(jnp.mean(x * x, axis=-1, keepdims=True) + EPS)


def _params(*sem):
    return pltpu.CompilerParams(dimension_semantics=sem, vmem_limit_bytes=VMEM_LIMIT)


def _mod_kernel(c_ref, w_ref, b_ref, o_ref):
    c = c_ref[...]
    o_ref[...] = jnp.dot(_silu(c), w_ref[...], precision=HI, preferred_element_type=F32) + b_ref[...]


def _mod_call(c_all, w_ada, b_ada):
    rows, d = c_all.shape
    n = w_ada.shape[1]
    return pl.pallas_call(
        _mod_kernel,
        grid=(n // d,),
        in_specs=[pl.BlockSpec((rows, d), lambda i: (0, 0)),
                  pl.BlockSpec((d, d), lambda i: (0, i)),
                  pl.BlockSpec((1, d), lambda i: (0, i))],
        out_specs=pl.BlockSpec((rows, d), lambda i: (0, i)),
        out_shape=jax.ShapeDtypeStruct((rows, n), F32),
        compiler_params=_params("arbitrary"),
        name="mod",
    )(c_all, w_ada, b_ada)


def _split3(f):
    hi = f.astype(BF16)
    r1 = f - hi.astype(F32)
    mid = r1.astype(BF16)
    lo = (r1 - mid.astype(F32)).astype(BF16)
    return hi, mid, lo


def _pre_prompt_kernel(x_ref, mod_ref, g_ref, wqkv_ref, wfh_ref, wfl_ref, bf_ref, wu_ref, tri_ref, paug_ref,
                       caug_ref, kt_ref, vt_ref, lft_ref, qa_ref, ka_ref, va_ref, u_ref, carry_ref,
                       *, n_heads, d_attn):
    @pl.when(pl.program_id(1) == 0)
    def _():
        carry_ref[...] = jnp.zeros_like(carry_ref)

    x = x_ref[0]
    tm = x.shape[0]
    h = _rms(x) * g_ref[...] * (1.0 + mod_ref[0, 1]) + mod_ref[0, 0]
    hb = h.astype(BF16)
    z = jnp.dot(hb, wqkv_ref[...], preferred_element_type=F32)
    u_ref[0] = jnp.dot(hb, wu_ref[...], preferred_element_type=F32)

    h_lo = (h - hb.astype(F32)).astype(BF16)
    fl = (jnp.dot(hb, wfh_ref[...], preferred_element_type=F32) + jnp.dot(hb, wfl_ref[...], preferred_element_type=F32)
          + jnp.dot(h_lo, wfh_ref[...], preferred_element_type=F32) + bf_ref[...])
    lane = lax.broadcasted_iota(jnp.int32, (tm, LANES), 1)
    logf = jnp.where(lane < n_heads, _log_sigmoid(fl), 0.0)
    logf_t = logf.T
    for pg in range(tm // LANES):
        lft_ref[0, pg] = logf_t[:n_heads, pg * LANES:(pg + 1) * LANES]
    tri_sum = jnp.dot(tri_ref[...], jnp.concatenate(_split3(logf), axis=1), preferred_element_type=F32)
    cum = tri_sum[:, :LANES] + tri_sum[:, LANES:2 * LANES] + tri_sum[:, 2 * LANES:] + carry_ref[...]
    carry_ref[...] = cum[tm - 1:tm, :]

    fs = jnp.concatenate(_split3(cum * LOG2_E), axis=1)
    aug = jnp.dot(fs, paug_ref[...], preferred_element_type=F32) + caug_ref[...]
    augq, augk = aug[:, :LANES], aug[:, LANES:]
    extra = (lane >= HEAD_DIM) & (lane < HEAD_DIM + AUG_LANES)
    low = lane < HEAD_DIM
    vone = jnp.where(lane == HEAD_DIM, 1.0, 0.0)
    scale = HEAD_DIM ** -0.5 * LOG2_E
    for j in range(n_heads // 2):
        zq = z[:, j * LANES:(j + 1) * LANES] * scale
        zk = z[:, d_attn + j * LANES:d_attn + (j + 1) * LANES]
        zv = z[:, 2 * d_attn + j * LANES:2 * d_attn + (j + 1) * LANES]
        for par in range(2):
            hh = 2 * j + par
            if par:
                zq, zk, zv = (pltpu.roll(a, HEAD_DIM, 1) for a in (zq, zk, zv))
            shift = (HEAD_DIM - AUG_STRIDE * hh) % LANES
            aq, ak = ((pltpu.roll(a, shift, 1) if shift else a) for a in (augq, augk))
            qa_ref[0, hh] = jnp.where(low, zq, jnp.where(extra, aq, 0.0)).astype(BF16)
            ka_ref[0, hh] = jnp.where(low, zk, jnp.where(extra, ak, 0.0)).astype(BF16)
            k_t = zk.T
            v_t = jnp.where(low, zv, vone).T
            va_ref[0, hh, 0] = v_t[:V_ROWS].astype(BF16)
            for pg in range(tm // LANES):
                kt_ref[0, pg, hh] = k_t[:HEAD_DIM, pg * LANES:(pg + 1) * LANES]
                vt_ref[0, pg, hh] = v_t[:HEAD_DIM, pg * LANES:(pg + 1) * LANES]


AUG_LANES = 6
AUG_STRIDE = 16


def _aug_constants(n_heads):
    assert n_heads * AUG_STRIDE <= LANES and AUG_LANES <= AUG_STRIDE
    place = np.zeros((3 * LANES, 2 * LANES), np.float32)
    const = np.zeros((1, 2 * LANES), np.float32)
    for h in range(n_heads):
        for piece in range(3):
            place[piece * LANES + h, h * AUG_STRIDE + piece] = 1.0
            const[0, h * AUG_STRIDE + 3 + piece] = 1.0
            const[0, LANES + h * AUG_STRIDE + piece] = 1.0
            place[piece * LANES + h, LANES + h * AUG_STRIDE + 3 + piece] = -1.0
    return jnp.asarray(place, BF16), jnp.asarray(const)


def _pre_prompt_call(x, mod_p, g_pre, w_qkv, w_f, b_f, w_u, n_heads, tm):
    b, s, d = x.shape
    d_attn = n_heads * HEAD_DIM
    d_ssm = w_u.shape[1]
    tri = jnp.asarray(np.tril(np.ones((tm, tm), np.float32)), BF16)
    w_f_hi = w_f.astype(BF16)
    w_f_lo = (w_f - w_f_hi.astype(F32)).astype(BF16)
    place, place_const = _aug_constants(n_heads)
    pages = tm // LANES
    const = lambda *shape: pl.BlockSpec(shape, lambda bi, ti: (0,) * len(shape))
    rows = lambda width: pl.BlockSpec((1, tm, width), lambda bi, ti: (bi, ti, 0))
    heads = pl.BlockSpec((1, n_heads, tm, LANES), lambda bi, ti: (bi, 0, ti, 0))
    heads_t = pl.BlockSpec((1, n_heads, 1, V_ROWS, tm), lambda bi, ti: (bi, 0, ti, 0, 0))
    paged = pl.BlockSpec((1, pages, n_heads, HEAD_DIM, LANES), lambda bi, ti: (bi, ti, 0, 0, 0))
    paged_shape = jax.ShapeDtypeStruct((b, s // LANES, n_heads, HEAD_DIM, LANES), F32)
    aug_shape = jax.ShapeDtypeStruct((b, n_heads, s, LANES), BF16)
    aug_t_shape = jax.ShapeDtypeStruct((b, n_heads, s // tm, V_ROWS, tm), BF16)
    return pl.pallas_call(
        functools.partial(_pre_prompt_kernel, n_heads=n_heads, d_attn=d_attn),
        grid=(b, s // tm),
        in_specs=[rows(d),
                  pl.BlockSpec((1, 6, 1, d), lambda bi, ti: (bi, 0, 0, 0)),
                  const(1, d), const(d, 3 * d_attn), const(d, LANES), const(d, LANES), const(1, LANES),
                  const(d, d_ssm), const(tm, tm), const(3 * LANES, 2 * LANES), const(1, 2 * LANES)],
        out_specs=[paged, paged, pl.BlockSpec((1, pages, n_heads, LANES), lambda bi, ti: (bi, ti, 0, 0)),
                   heads, heads, heads_t, rows(d_ssm)],
        out_shape=[paged_shape, paged_shape, jax.ShapeDtypeStruct((b, s // LANES, n_heads, LANES), F32),
                   aug_shape, aug_shape, aug_t_shape, jax.ShapeDtypeStruct((b, s, d_ssm), F32)],
        scratch_shapes=[pltpu.VMEM((1, LANES), F32)],
        compiler_params=_params("arbitrary", "arbitrary"),
        name="pre_prompt",
    )(x, mod_p, g_pre, w_qkv, w_f_hi, w_f_lo, b_f, w_u, tri, place, place_const)


def _pre_sample_kernel(x_ref, mod_ref, g_ref, wqkv_ref, wf_ref, bf_ref, wu_ref, z_ref, lf_ref, u_ref):
    x = x_ref[...]
    h = _rms(x) * g_ref[...] * (1.0 + mod_ref[0, 1]) + mod_ref[0, 0]
    hb = h.astype(BF16)
    z_ref[...] = jnp.dot(hb, wqkv_ref[...], preferred_element_type=F32)
    u_ref[...] = jnp.dot(hb, wu_ref[...], preferred_element_type=F32)
    fl = jnp.dot(h, wf_ref[...], precision=HI, preferred_element_type=F32) + bf_ref[...]
    lf_ref[...] = _log_sigmoid(fl)


def _pre_sample_call(x, mod_s, g_pre, w_qkv, w_f, b_f, w_u):
    rows = x.shape[0]
    return pl.pallas_call(
        _pre_sample_kernel,
        out_shape=[jax.ShapeDtypeStruct((rows, w_qkv.shape[1]), F32),
                   jax.ShapeDtypeStruct((rows, LANES), F32),
                   jax.ShapeDtypeStruct((rows, w_u.shape[1]), F32)],
        compiler_params=pltpu.CompilerParams(vmem_limit_bytes=VMEM_LIMIT),
        name="pre_sample",
    )(x, mod_s, g_pre, w_qkv, w_f, b_f, w_u)


FLASH_HEADS = 4


def _flash_kernel(q_ref, k_ref, vt_ref, o_ref, st_ref, bmax_ref, m_ref, acc_ref, *, tq):
    qi = pl.program_id(2)
    heads = q_ref.shape[1]

    def put_scores(kb, buf):
        off = pl.multiple_of(kb * tq, tq)
        for h in range(heads):
            st = lax.dot_general(k_ref[0, h, pl.ds(off, tq), :], q_ref[0, h], (((1,), (1,)), ((), ())),
                                 preferred_element_type=F32)
            st_ref[h, buf] = st
            bmax_ref[h, buf] = jnp.max(st, axis=0, keepdims=True)

    def update(kb, buf, diagonal):
        for h in range(heads):
            st = st_ref[h, buf]
            if diagonal:
                key = lax.broadcasted_iota(jnp.int32, st.shape, 0)
                qry = lax.broadcasted_iota(jnp.int32, st.shape, 1)
                st = jnp.where(key <= qry, st, NEG)
                block_max = jnp.max(st, axis=0, keepdims=True)
            else:
                block_max = bmax_ref[h, buf]
            m = m_ref[h]
            m_new = jnp.maximum(m, block_max)
            acc_ref[h] = acc_ref[h] * jnp.exp2(m - m_new) + jnp.dot(
                vt_ref[0, h, kb], jnp.exp2(st - m_new).astype(BF16), preferred_element_type=F32)
            m_ref[h] = m_new

    def finish():
        for h in range(heads):
            acc = acc_ref[h]
            o = acc[:HEAD_DIM] * (1.0 / acc[HEAD_DIM:HEAD_DIM + 1, :])
            o_ref[0, h] = jnp.concatenate([o, jnp.zeros((LANES - HEAD_DIM, tq), F32)], axis=0).T.astype(BF16)

    m_ref[...] = jnp.full_like(m_ref, NEG)
    acc_ref[...] = jnp.zeros_like(acc_ref)
    put_scores(0, 0)

    def pair(j, carry):
        put_scores(2 * j + 1, 1)
        update(2 * j, 0, False)
        put_scores(2 * j + 2, 0)
        update(2 * j + 1, 1, False)
        return carry

    lax.fori_loop(0, qi // 2, pair, 0)

    @pl.when(qi % 2 == 1)
    def _():
        put_scores(qi, 1)
        update(qi - 1, 0, False)
        update(qi, 1, True)
        finish()

    @pl.when(qi % 2 == 0)
    def _():
        update(qi, 0, True)
        finish()


def _flash_call(qa, ka, vat, tq):
    b, h, s, _ = qa.shape
    assert vat.shape[-1] == tq, "value chunks are laid out per key block"
    nh = math.gcd(FLASH_HEADS, h)
    qspec = pl.BlockSpec((1, nh, tq, LANES), lambda bi, hi, qi: (bi, hi, qi, 0))
    once = pl.Buffered(1)
    kspec = pl.BlockSpec((1, nh, s, LANES), lambda bi, hi, qi: (bi, hi, 0, 0), pipeline_mode=once)
    vspec = pl.BlockSpec((1, nh, s // tq, V_ROWS, tq), lambda bi, hi, qi: (bi, hi, 0, 0, 0), pipeline_mode=once)
    return pl.pallas_call(
        functools.partial(_flash_kernel, tq=tq),
        grid=(b, h // nh, s // tq),
        in_specs=[qspec, kspec, vspec],
        out_specs=qspec,
        out_shape=jax.ShapeDtypeStruct((b, h, s, LANES), BF16),
        scratch_shapes=[pltpu.VMEM((nh, 2, tq, tq), F32), pltpu.VMEM((nh, 2, 1, tq), F32),
                        pltpu.VMEM((nh, 1, tq), F32),
                        pltpu.VMEM((nh, V_ROWS, tq), F32)],
        compiler_params=_params("arbitrary", "arbitrary", "arbitrary"),
        name="flash",
    )(qa, ka, vat)


def _cmul(ar, ai, br, bi):
    return ar * br - ai * bi, ar * bi + ai * br


def _ssm_discretize(a_re, a_im, log_dt, b_re, b_im, c_re, c_im):
    ar, ai = a_re.astype(F32), a_im.astype(F32)
    dt = jnp.exp(log_dt.astype(F32))[:, None]
    lam_dt = (ar * dt, ai * dt)
    mag = jnp.exp(lam_dt[0])
    lam_bar = (mag * jnp.cos(lam_dt[1]), mag * jnp.sin(lam_dt[1]))
    den = ar * ar + ai * ai
    nr, ni = lam_bar[0] - 1.0, lam_bar[1]
    coef = ((nr * ar + ni * ai) / den, (ni * ar - nr * ai) / den)
    b_bar = _cmul(coef[0][..., None], coef[1][..., None], b_re.astype(F32), b_im.astype(F32))
    c = (c_re.astype(F32), c_im.astype(F32))
    return lam_dt, lam_bar, b_bar, c


def _ssm_chunk_operators(lam_dt, b_bar, c, chunk):
    g, p = lam_dt[0].shape
    nt = g // GROUPS_PER_TILE
    steps = jnp.arange(chunk + 1, dtype=F32)[:, None, None]
    mag = jnp.exp(steps * lam_dt[0][None])
    pw = (mag * jnp.cos(steps * lam_dt[1][None]), mag * jnp.sin(steps * lam_dt[1][None]))
    gpt = GROUPS_PER_TILE
    group_of = lambda cols, width: jnp.asarray(
        (np.arange(cols)[None, :] // width) % gpt == np.arange(gpt)[:, None], F32)
    pb = _cmul(pw[0][:chunk, :, :, None], pw[1][:chunk, :, :, None], b_bar[0][None], b_bar[1][None])
    kmat = (jnp.einsum("ghp,dgpk->dkgh", c[0], pb[0], precision=HI)
            - jnp.einsum("ghp,dgpk->dkgh", c[1], pb[1], precision=HI))
    kc = kmat.reshape(chunk, SSM_GROUP, nt, LANES).transpose(2, 0, 1, 3)
    lag_blocks = (kc[:, :, None] * group_of(LANES, SSM_GROUP)[None, None, :, None, :]
                  ).reshape(nt, chunk, LANES, LANES).astype(BF16)
    rev = chunk - 1 - np.arange(chunk)
    wb = _cmul(pw[0][rev][..., None], pw[1][rev][..., None], b_bar[0][None], b_bar[1][None])
    wsm = jnp.stack([part.reshape(chunk, nt, gpt, p, SSM_GROUP) for part in wb])
    wsm = wsm.transpose(2, 1, 5, 0, 3, 4).reshape(nt, chunk, SSM_GROUP, 2 * gpt * p)
    to_state = (wsm[:, :, None] * group_of(2 * gpt * p, p)[None, None, :, None, :]
                ).reshape(nt, chunk * LANES, 2 * gpt * p).astype(BF16)
    cp = _cmul(c[0][None], c[1][None], pw[0][1:chunk + 1][:, :, None, :], pw[1][1:chunk + 1][:, :, None, :])
    csm = jnp.stack([part.reshape(chunk, nt, gpt, SSM_GROUP, p) for part in (cp[0], -cp[1])])
    csm = csm.transpose(2, 0, 5, 1, 3, 4).reshape(nt, 2, p, chunk * LANES)
    from_state = (csm[:, :, None] * group_of(chunk * LANES, SSM_GROUP)[None, None, :, None, :]
                  ).reshape(nt, 2 * gpt * p, chunk * LANES).astype(BF16)
    dec_re = pw[0][chunk].reshape(nt, 1, gpt * p)
    dec_im = pw[1][chunk].reshape(nt, 1, gpt * p)
    return lag_blocks, to_state, from_state, dec_re, dec_im


def _ssm_kernel(u_ref, lag_ref, g_ref, c_ref, are_ref, aim_ref, d_ref, y_ref, hfin_ref, xin_ref, xs_ref, st_ref,
                toep_ref, *, chunk, n_chunks):
    r = pl.program_id(2)

    @pl.when((pl.program_id(1) == 0) & (r == 0))
    def _():
        for l_in in range(chunk):
            for l_out in range(chunk):
                blk = lag_ref[0, l_out - l_in] if l_out >= l_in else jnp.zeros((LANES, LANES), BF16)
                toep_ref[l_in * LANES:(l_in + 1) * LANES, l_out * LANES:(l_out + 1) * LANES] = blk

    @pl.when(r == 0)
    def _():
        st_ref[...] = jnp.zeros_like(st_ref)

    half = st_ref.shape[1] // 2
    pieces = [u_ref[0, pl.ds(l, n_chunks, stride=chunk), :] for l in range(chunk)]
    u2 = jnp.concatenate([pc.astype(BF16) for pc in pieces], axis=1)
    xin_ref[...] = jnp.dot(u2, g_ref[0], preferred_element_type=F32)
    ar = are_ref[0]
    ai = aim_ref[0]

    def body(i, carry):
        xr, xi = carry
        base = pl.multiple_of(i * 8, 8)
        blk = xin_ref[pl.ds(base, 8), :]
        rows_r, rows_i = [], []
        for rr in range(8):
            rows_r.append(xr)
            rows_i.append(xi)
            xr, xi = (ar * xr - ai * xi + blk[rr:rr + 1, :half],
                      ar * xi + ai * xr + blk[rr:rr + 1, half:])
        xs_ref[pl.ds(base, 8), :] = jnp.concatenate(
            [jnp.concatenate(rows_r, axis=0), jnp.concatenate(rows_i, axis=0)], axis=1)
        return xr, xi

    xr, xi = lax.fori_loop(0, n_chunks // 8, body, (st_ref[:, :half], st_ref[:, half:]), unroll=True)
    st_ref[...] = jnp.concatenate([xr, xi], axis=1)
    y2 = (jnp.dot(u2, toep_ref[...], preferred_element_type=F32)
          + jnp.dot(xs_ref[...].astype(BF16), c_ref[0], preferred_element_type=F32))
    dd = d_ref[0]
    for l in range(chunk):
        y_ref[0, pl.ds(l, n_chunks, stride=chunk), :] = y2[:, l * LANES:(l + 1) * LANES] + dd * pieces[l]

    @pl.when(r == pl.num_programs(2) - 1)
    def _():
        hfin_ref[0, 0] = st_ref[...]


def _ssm_call(u, lag_blocks, to_state, from_state, dec_re, dec_im, d_skip, rows_per_step):
    b, s, d_ssm = u.shape
    nt = d_ssm // LANES
    chunk = lag_blocks.shape[1]
    n_chunks = rows_per_step // chunk
    nstate = to_state.shape[2]
    tile = lambda *shape: pl.BlockSpec((1,) + shape, lambda j, bi, r: (j, 0, 0))
    useq = pl.BlockSpec((1, rows_per_step, LANES), lambda j, bi, r: (bi, r, j))
    return pl.pallas_call(
        functools.partial(_ssm_kernel, chunk=chunk, n_chunks=n_chunks),
        grid=(nt, b, s // rows_per_step),
        in_specs=[useq, pl.BlockSpec((1, chunk, LANES, LANES), lambda j, bi, r: (j, 0, 0, 0)),
                  tile(chunk * LANES, nstate),
                  tile(nstate, chunk * LANES), tile(1, nstate // 2), tile(1, nstate // 2), tile(1, LANES)],
        out_specs=[useq, pl.BlockSpec((1, 1, 1, nstate), lambda j, bi, r: (j, bi, 0, 0))],
        out_shape=[jax.ShapeDtypeStruct((b, s, d_ssm), F32), jax.ShapeDtypeStruct((nt, b, 1, nstate), F32)],
        scratch_shapes=[pltpu.VMEM((n_chunks, nstate), F32), pltpu.VMEM((n_chunks, nstate), F32),
                        pltpu.VMEM((1, nstate), F32), pltpu.VMEM((chunk * LANES, chunk * LANES), BF16)],
        compiler_params=_params("arbitrary", "arbitrary", "arbitrary"),
        name="ssm",
    )(u, lag_blocks, to_state, from_state, dec_re, dec_im, d_skip)


def _ssm_step_kernel(u_ref, hre_ref, him_ref, bre_ref, bim_ref, lre_ref, lim_ref, cre_ref, cim_ref, d_ref,
                     y_ref, xre_ref, xim_ref):
    u = u_ref[...]
    hr, hi = hre_ref[...], him_ref[...]
    lr, li = lre_ref[...], lim_ref[...]
    xr = lr * hr - li * hi + jnp.dot(u, bre_ref[...], precision=HI, preferred_element_type=F32)
    xi = lr * hi + li * hr + jnp.dot(u, bim_ref[...], precision=HI, preferred_element_type=F32)
    xre_ref[...] = xr
    xim_ref[...] = xi
    y_ref[...] = (jnp.dot(xr, cre_ref[...], precision=HI, preferred_element_type=F32)
                  - jnp.dot(xi, cim_ref[...], precision=HI, preferred_element_type=F32) + d_ref[...] * u)


def _ssm_step_call(u, h_re, h_im, lam_bar, b_bar, c, d_skip):
    rows, d_ssm = u.shape
    g, p = lam_bar[0].shape
    eye = jnp.eye(g, dtype=F32)
    bmat = [jnp.einsum("gpk,gf->gkfp", part, eye, precision=HI).reshape(d_ssm, g * p) for part in b_bar]
    cmat = [jnp.einsum("ghp,gf->gpfh", part, eye, precision=HI).reshape(g * p, d_ssm) for part in c]
    return pl.pallas_call(
        _ssm_step_kernel,
        out_shape=[jax.ShapeDtypeStruct((rows, d_ssm), F32), jax.ShapeDtypeStruct((rows, g * p), F32),
                   jax.ShapeDtypeStruct((rows, g * p), F32)],
        compiler_params=pltpu.CompilerParams(vmem_limit_bytes=VMEM_LIMIT),
        name="ssm_step",
    )(u, h_re.reshape(rows, g * p), h_im.reshape(rows, g * p), bmat[0], bmat[1],
      lam_bar[0].reshape(1, g * p), lam_bar[1].reshape(1, g * p), cmat[0], cmat[1], d_skip.reshape(1, d_ssm))


DEC_PAGES = 16


def _dec_kernel(pt_ref, q_ref, kn_ref, qt_ref, vnt_ref, lfn_ref, sli_ref, *rest, n_heads, pages):
    del pt_ref
    kt_refs, vt_refs, lf_refs = rest[:pages], rest[pages:2 * pages], rest[2 * pages:3 * pages]
    o_ref, qb_ref, m_ref, l_ref, acc_ref, carry_ref = rest[3 * pages:]
    step = pl.program_id(1)
    scale = HEAD_DIM ** -0.5

    @pl.when(step == 0)
    def _():
        qt = qt_ref[0] * scale
        vnt = vnt_ref[0]
        lane = lax.broadcasted_iota(jnp.int32, (HEAD_DIM, LANES), 1)
        for h in range(n_heads):
            qb_ref[h] = jnp.broadcast_to(qt[:, h:h + 1], (HEAD_DIM, LANES))
            acc_ref[h] = jnp.where(lane == 0, jnp.broadcast_to(vnt[:, h:h + 1], (HEAD_DIM, LANES)), 0.0)
        m_ref[...] = jnp.sum(q_ref[0] * kn_ref[0], axis=1, keepdims=True) * scale
        l_ref[...] = jnp.ones_like(l_ref)
        carry_ref[...] = lfn_ref[0]

    lf_all = jnp.concatenate([r[0, 0] for r in lf_refs], axis=0)
    sli = sli_ref[...]
    incl = sum(jnp.dot(piece, sli, preferred_element_type=F32) for piece in _split3(lf_all))
    run = carry_ref[...]
    scores = [None] * pages
    for i in reversed(range(pages)):
        rows = slice(i * n_heads, (i + 1) * n_heads)
        bias = incl[rows] - lf_all[rows] + run
        run = run + incl[rows][:, 0:1]
        qk = [jnp.sum(kt_refs[i][0, 0, h] * qb_ref[h], axis=0, keepdims=True) for h in range(n_heads)]
        scores[i] = jnp.concatenate(qk, axis=0) + bias
    carry_ref[...] = run

    m_old = m_ref[...]
    m_new = jnp.maximum(m_old, jnp.max(functools.reduce(jnp.maximum, scores), axis=1, keepdims=True))
    corr = jnp.exp(m_old - m_new)
    probs = [jnp.exp(s - m_new) for s in scores]
    l_ref[...] = l_ref[...] * corr + jnp.sum(functools.reduce(jnp.add, probs), axis=1, keepdims=True)
    m_ref[...] = m_new
    for h in range(n_heads):
        a = acc_ref[h] * corr[h:h + 1, :]
        for i in range(pages):
            a = a + vt_refs[i][0, 0, h] * probs[i][h:h + 1, :]
        acc_ref[h] = a

    @pl.when(step == pl.num_programs(1) - 1)
    def _():
        inv = 1.0 / l_ref[...]
        ones = jnp.ones((8, LANES), BF16)
        for h in range(n_heads):
            sums = sum(lax.dot_general(ones, piece, (((1,), (1,)), ((), ())), preferred_element_type=F32)
                       for piece in _split3(acc_ref[h]))
            o_ref[0, h:h + 1, :] = sums[0:1, :] * inv[h:h + 1, :]


def _dec_call(page_table, q, kn, qt, vnt, lfn, cache_k, cache_v, cache_logf, pages):
    db, n_pages = page_table.shape
    _, _, page, n_heads, _ = cache_k.shape
    assert page == LANES and n_pages % pages == 0
    kt = jnp.transpose(cache_k, (0, 1, 3, 4, 2))
    vt = jnp.transpose(cache_v, (0, 1, 3, 4, 2))
    lft = jnp.transpose(cache_logf, (0, 1, 3, 2))
    sli = jnp.asarray(np.tril(np.ones((page, page), np.float32)), BF16)
    pt_flat = page_table.reshape(-1)

    def phys(i):
        return lambda bi, st, pt: pt[bi * n_pages + n_pages - pages * (st + 1) + i]

    per_b = lambda *shape: pl.BlockSpec((1,) + shape, lambda bi, st, pt: (bi,) + (0,) * len(shape))
    kv_spec = lambda i: pl.BlockSpec((1, 1, n_heads, HEAD_DIM, page),
                                     lambda bi, st, pt, f=phys(i): (0, f(bi, st, pt), 0, 0, 0))
    lf_spec = lambda i: pl.BlockSpec((1, 1, n_heads, page), lambda bi, st, pt, f=phys(i): (0, f(bi, st, pt), 0, 0))
    grid_spec = pltpu.PrefetchScalarGridSpec(
        num_scalar_prefetch=1,
        grid=(db, n_pages // pages),
        in_specs=([per_b(n_heads, HEAD_DIM), per_b(n_heads, HEAD_DIM), per_b(HEAD_DIM, n_heads),
                   per_b(HEAD_DIM, n_heads), per_b(n_heads, 1), pl.BlockSpec((page, page), lambda bi, st, pt: (0, 0))]
                  + [kv_spec(i) for i in range(pages)] + [kv_spec(i) for i in range(pages)]
                  + [lf_spec(i) for i in range(pages)]),
        out_specs=per_b(n_heads, HEAD_DIM),
        scratch_shapes=[pltpu.VMEM((n_heads, HEAD_DIM, page), F32), pltpu.VMEM((n_heads, 1), F32),
                        pltpu.VMEM((n_heads, 1), F32), pltpu.VMEM((n_heads, HEAD_DIM, page), F32),
                        pltpu.VMEM((n_heads, 1), F32)])
    return pl.pallas_call(
        functools.partial(_dec_kernel, n_heads=n_heads, pages=pages),
        grid_spec=grid_spec,
        out_shape=jax.ShapeDtypeStruct((db, n_heads, HEAD_DIM), F32),
        compiler_params=_params("arbitrary", "arbitrary"),
        name="dec",
    )(pt_flat, q, kn, qt, vnt, lfn, sli, *([kt] * pages), *([vt] * pages), *([lft] * pages))


def _post_kernel(x_ref, o_ref, ys_ref, mod_ref, wglu_ref, bglu_ref, gatt_ref, gssm_ref, wout_ref, gpm_ref,
                 gpf_ref, wgu_ref, wdn_ref, gpo_ref, y_ref, *, n_heads, ff_chunk):
    x = x_ref[0]
    pairs = []
    for j in range(n_heads // 2):
        even = o_ref[0, 2 * j].astype(F32)
        odd = o_ref[0, 2 * j + 1].astype(F32)
        pairs.append(even + pltpu.roll(odd, HEAD_DIM, 1))
    attn = jnp.concatenate(pairs, axis=1)
    gl = _gelu_tanh(ys_ref[0])
    gl = gl * _sigmoid(jnp.dot(gl.astype(BF16), wglu_ref[...], preferred_element_type=F32) + bglu_ref[...])
    mix = jnp.concatenate([_rms(attn) * gatt_ref[...], _rms(gl) * gssm_ref[...]], axis=1)
    mo = jnp.dot(mix.astype(BF16), wout_ref[...], preferred_element_type=F32)
    x1 = x + mod_ref[0, 2] * (_rms(mo) * gpm_ref[...])
    h2 = (_rms(x1) * gpf_ref[...] * (1.0 + mod_ref[0, 4]) + mod_ref[0, 3]).astype(BF16)
    d_ff = wdn_ref.shape[0]
    acc = jnp.zeros_like(x)
    for c0 in range(0, d_ff, ff_chunk):
        gate = jnp.dot(h2, wgu_ref[:, c0:c0 + ff_chunk], preferred_element_type=F32)
        up = jnp.dot(h2, wgu_ref[:, d_ff + c0:d_ff + c0 + ff_chunk], preferred_element_type=F32)
        acc = acc + jnp.dot((_silu(gate) * up).astype(BF16), wdn_ref[c0:c0 + ff_chunk, :],
                            preferred_element_type=F32)
    y_ref[0] = x1 + mod_ref[0, 5] * (_rms(acc) * gpo_ref[...])


def _post_call(x, o, ys, mod, w_glu, b_glu, g_attn, g_ssm, w_out, g_post_mix, g_pre_ffn, w_gate_up, w_down,
               g_post_ffn, tm, ff_chunk):
    nb, s, d = x.shape
    n_heads = o.shape[1]
    d_ssm = ys.shape[2]
    mod_rows = mod.shape[2]
    const = lambda a: pl.BlockSpec(a.shape, lambda bi, ti: (0,) * a.ndim, pipeline_mode=pl.Buffered(1))
    rows = lambda width: pl.BlockSpec((1, tm, width), lambda bi, ti: (bi, ti, 0))
    mod_spec = (pl.BlockSpec((1, 6, 1, d), lambda bi, ti: (bi, 0, 0, 0)) if mod_rows == 1
                else pl.BlockSpec((1, 6, tm, d), lambda bi, ti: (bi, 0, ti, 0)))
    weights = (w_glu, b_glu, g_attn, g_ssm, w_out, g_post_mix, g_pre_ffn, w_gate_up, w_down, g_post_ffn)
    return pl.pallas_call(
        functools.partial(_post_kernel, n_heads=n_heads, ff_chunk=ff_chunk),
        grid=(nb, s // tm),
        in_specs=[rows(d), pl.BlockSpec((1, n_heads, tm, LANES), lambda bi, ti: (bi, 0, ti, 0)), rows(d_ssm),
                  mod_spec] + [const(w) for w in weights],
        out_specs=rows(d),
        out_shape=jax.ShapeDtypeStruct((nb, s, d), F32),
        compiler_params=_params("arbitrary", "arbitrary"),
        name="post",
    )(x, o, ys, mod, *weights)


def _row(v):
    return v.reshape(1, -1).astype(F32)


def _layer(xp, xs, cp, cs, cache_k, cache_v, cache_logf, h_re, h_im, page_table, w):
    b, s, d = xp.shape
    db = xs.shape[0]
    n_heads = w["b_f"].shape[0]
    d_attn = n_heads * HEAD_DIM
    d_ff = w["w_down"].shape[0]
    g, p = w["a_re"].shape
    d_ssm = g * SSM_GROUP

    w_in = w["w_in"]
    w_qkv = w_in[:, :3 * d_attn].astype(BF16)
    w_f = jnp.pad(w_in[:, 3 * d_attn:3 * d_attn + n_heads].astype(F32), ((0, 0), (0, LANES - n_heads)))
    b_f = jnp.pad(_row(w["b_f"]), ((0, 0), (0, LANES - n_heads)))
    w_u = w_in[:, 3 * d_attn + n_heads:].astype(BF16)
    post_w = (w["w_glu"].astype(BF16), _row(w["b_glu"]), _row(w["g_attn_out"]), _row(w["g_ssm_out"]),
              w["w_out"].astype(BF16), _row(w["g_post_mix"]), _row(w["g_pre_ffn"]), w["w_gate_up"].astype(BF16),
              w["w_down"].astype(BF16), _row(w["g_post_ffn"]))
    g_pre = _row(w["g_pre_mix"])

    n_cond = b + db
    pad_rows = -n_cond % 8
    c_all = jnp.pad(jnp.concatenate([cp, cs], axis=0).astype(F32), ((0, pad_rows), (0, 0)))
    mod = _mod_call(c_all, w["w_ada"].astype(F32), _row(w["b_ada"]))
    mod_p = mod[:b].reshape(b, 6, 1, d)
    mod_s = mod[b:n_cond].reshape(db, 6, d).transpose(1, 0, 2)[None]

    tm = min(ROW_TILE, s)
    assert s % tm == 0 and tm % LANES == 0 and cache_k.shape[2] == LANES
    k_p, v_p, lf_p, qa, ka, va, u_p = _pre_prompt_call(xp, mod_p, g_pre, w_qkv, w_f, b_f, w_u, n_heads, tm)
    o_p = _flash_call(qa, ka, va, tm)
    lam_dt, lam_bar, b_bar, c = _ssm_discretize(w["a_re"], w["a_im"], w["log_dt"], w["b_re"], w["b_im"],
                                                w["c_re"], w["c_im"])
    lag_blocks, to_state, from_state, dec_re, dec_im = _ssm_chunk_operators(lam_dt, b_bar, c, SSM_CHUNK)
    nt = d_ssm // LANES
    ys_p, hfin = _ssm_call(u_p, lag_blocks, to_state, from_state, dec_re, dec_im,
                           w["d_skip"].astype(F32).reshape(nt, 1, LANES), math.gcd(SSM_ROWS, s))
    y_p = _post_call(xp, o_p, ys_p, mod_p, *post_w, tm=tm, ff_chunk=math.gcd(FFN_CHUNK, d_ff))
    half = hfin.shape[-1] // 2
    hre_p = hfin[:, :, 0, :half].transpose(1, 0, 2).reshape(b, g, p)
    him_p = hfin[:, :, 0, half:].transpose(1, 0, 2).reshape(b, g, p)

    xs2 = xs.reshape(db, d)
    z_s, lf_s, u_s = _pre_sample_call(xs2, mod_s, g_pre, w_qkv, w_f, b_f, w_u)
    q_s = z_s[:, :d_attn].reshape(db, n_heads, HEAD_DIM)
    k_s = z_s[:, d_attn:2 * d_attn].reshape(db, n_heads, HEAD_DIM)
    v_s = z_s[:, 2 * d_attn:].reshape(db, n_heads, HEAD_DIM)
    lfn = lf_s[:, :n_heads]
    n_pages = page_table.shape[1]
    o_s = _dec_call(page_table, q_s, k_s, q_s.transpose(0, 2, 1), v_s.transpose(0, 2, 1),
                    lfn.reshape(db, n_heads, 1), cache_k, cache_v, cache_logf, math.gcd(DEC_PAGES, n_pages))
    o_s = jnp.pad(o_s.transpose(1, 0, 2), ((0, 0), (0, 0), (0, LANES - HEAD_DIM))).astype(BF16)[None]
    ys_s, hre_s, him_s = _ssm_step_call(u_s, h_re, h_im, lam_bar, b_bar, c, w["d_skip"].astype(F32))
    y_s = _post_call(xs2[None], o_s, ys_s[None], mod_s, *post_w, tm=db, ff_chunk=math.gcd(FFN_CHUNK, d_ff))

    return dict(
        y_p=y_p, y_s=y_s.reshape(db, 1, d),
        k_p=k_p.transpose(0, 1, 4, 2, 3), v_p=v_p.transpose(0, 1, 4, 2, 3),
        f_p=lf_p.transpose(0, 1, 3, 2), r_p=hre_p, i_p=him_p,
        k_s=k_s.reshape(db, 1, n_heads, HEAD_DIM), v_s=v_s.reshape(db, 1, n_heads, HEAD_DIM),
        f_s=lfn.reshape(db, 1, n_heads), r_s=hre_s.reshape(db, g, p), i_s=him_s.reshape(db, g, p))


def kernel(x_prompt, x_sample, c_prompt, c_sample, cache_k, cache_v, cache_logf, state_ssm_re, state_ssm_im,
           page_table, w_ada, b_ada, g_pre_mix, g_post_mix, g_pre_ffn, g_post_ffn, w_in, b_f, a_re, a_im,
           log_dt, b_re, b_im, c_re, c_im, d_skip, w_glu, b_glu, g_attn_out, g_ssm_out, w_out, w_gate_up, w_down):
    depth = w_in.shape[0]
    assert depth == 1 and x_sample.shape[1] == 1, "single layer, one new token per sequence"
    weights = dict(w_ada=w_ada, b_ada=b_ada, g_pre_mix=g_pre_mix, g_post_mix=g_post_mix, g_pre_ffn=g_pre_ffn,
                   g_post_ffn=g_post_ffn, w_in=w_in, b_f=b_f, a_re=a_re, a_im=a_im, log_dt=log_dt, b_re=b_re,
                   b_im=b_im, c_re=c_re, c_im=c_im, d_skip=d_skip, w_glu=w_glu, b_glu=b_glu,
                   g_attn_out=g_attn_out, g_ssm_out=g_ssm_out, w_out=w_out, w_gate_up=w_gate_up, w_down=w_down)
    w0 = {name: val[0] for name, val in weights.items()}
    r = _layer(x_prompt.astype(F32), x_sample.astype(F32), c_prompt, c_sample, cache_k, cache_v, cache_logf,
               state_ssm_re[0], state_ssm_im[0], page_table, w0)
    stack = lambda a: a[None]
    return (r["y_p"].astype(x_prompt.dtype), r["y_s"].astype(x_sample.dtype),
            stack(r["k_p"]), stack(r["v_p"]), stack(r["f_p"]), stack(r["r_p"]), stack(r["i_p"]),
            stack(r["k_s"]), stack(r["v_s"]), stack(r["f_s"]), stack(r["r_s"]), stack(r["i_s"]))
```

```python
import functools
import math

import numpy as np
import jax
import jax.numpy as jnp
from jax import lax
from jax.experimental import pallas as pl
from jax.experimental.pallas import tpu as pltpu

F32 = jnp.float32
BF16 = jnp.bfloat16
HI = lax.Precision.HIGHEST
EPS = 1e-6
NEG = -1e30
HEAD_DIM = 64
SSM_GROUP = 16
LANES = 128
GROUPS_PER_TILE = LANES // SSM_GROUP
VMEM_LIMIT = 56 * 1024 * 1024
ROW_TILE = 512
SSM_ROWS = 2048
FFN_CHUNK = 256
SSM_CHUNK = 8
V_ROWS = 80
SQRT_2_OVER_PI = math.sqrt(2.0 / math.pi)
LOG2_E = math.log2(math.e)


def _sigmoid(x):
    return 1.0 / (1.0 + jnp.exp(-x))


def _silu(x):
    return x * _sigmoid(x)


def _log_sigmoid(x):
    return jnp.minimum(x, 0.0) - jnp.log1p(jnp.exp(-jnp.abs(x)))


def _gelu_tanh(x):
    return x * (0.5 * (1.0 + jnp.tanh(SQRT_2_OVER_PI * (x + 0.044715 * (x * x * x)))))


def _rms(x):
    return x * lax.rsqrt(jnp.mean(x * x, axis=-1, keepdims=True) + EPS)


def _params(*sem):
    return pltpu.CompilerParams(dimension_semantics=sem, vmem_limit_bytes=VMEM_LIMIT)


def _mod_kernel(c_ref, w_ref, b_ref, o_ref):
    c = c_ref[...]
    o_ref[...] = jnp.dot(_silu(c), w_ref[...], precision=HI, preferred_element_type=F32) + b_ref[...]


def _mod_call(c_all, w_ada, b_ada):
    rows, d = c_all.shape
    n = w_ada.shape[1]
    return pl.pallas_call(
        _mod_kernel,
        grid=(n // d,),
        in_specs=[pl.BlockSpec((rows, d), lambda i: (0, 0)),
                  pl.BlockSpec((d, d), lambda i: (0, i)),
                  pl.BlockSpec((1, d), lambda i: (0, i))],
        out_specs=pl.BlockSpec((rows, d), lambda i: (0, i)),
        out_shape=jax.ShapeDtypeStruct((rows, n), F32),
        compiler_params=_params("arbitrary"),
        name="mod",
    )(c_all, w_ada, b_ada)


def _split3(f):
    hi = f.astype(BF16)
    r1 = f - hi.astype(F32)
    mid = r1.astype(BF16)
    lo = (r1 - mid.astype(F32)).astype(BF16)
    return hi, mid, lo


def _pre_prompt_kernel(x_ref, mod_ref, g_ref, wqkv_ref, wfh_ref, wfl_ref, bf_ref, wu_ref, tri_ref, paug_ref,
                       caug_ref, kt_ref, vt_ref, lft_ref, qa_ref, ka_ref, va_ref, u_ref, carry_ref,
                       *, n_heads, d_attn):
    @pl.when(pl.program_id(1) == 0)
    def _():
        carry_ref[...] = jnp.zeros_like(carry_ref)

    x = x_ref[0]
    tm = x.shape[0]
    h = _rms(x) * g_ref[...] * (1.0 + mod_ref[0, 1]) + mod_ref[0, 0]
    hb = h.astype(BF16)
    z = jnp.dot(hb, wqkv_ref[...], preferred_element_type=F32)
    u_ref[0] = jnp.dot(hb, wu_ref[...], preferred_element_type=F32)

    h_lo = (h - hb.astype(F32)).astype(BF16)
    fl = (jnp.dot(hb, wfh_ref[...], preferred_element_type=F32) + jnp.dot(hb, wfl_ref[...], preferred_element_type=F32)
          + jnp.dot(h_lo, wfh_ref[...], preferred_element_type=F32) + bf_ref[...])
    lane = lax.broadcasted_iota(jnp.int32, (tm, LANES), 1)
    logf = jnp.where(lane < n_heads, _log_sigmoid(fl), 0.0)
    logf_t = logf.T
    for pg in range(tm // LANES):
        lft_ref[0, pg] = logf_t[:n_heads, pg * LANES:(pg + 1) * LANES]
    tri_sum = jnp.dot(tri_ref[...], jnp.concatenate(_split3(logf), axis=1), preferred_element_type=F32)
    cum = tri_sum[:, :LANES] + tri_sum[:, LANES:2 * LANES] + tri_sum[:, 2 * LANES:] + carry_ref[...]
    carry_ref[...] = cum[tm - 1:tm, :]

    fs = jnp.concatenate(_split3(cum * LOG2_E), axis=1)
    aug = jnp.dot(fs, paug_ref[...], preferred_element_type=F32) + caug_ref[...]
    augq, augk = aug[:, :LANES], aug[:, LANES:]
    extra = (lane >= HEAD_DIM) & (lane < HEAD_DIM + AUG_LANES)
    low = lane < HEAD_DIM
    vone = jnp.where(lane == HEAD_DIM, 1.0, 0.0)
    scale = HEAD_DIM ** -0.5 * LOG2_E
    for j in range(n_heads // 2):
        zq = z[:, j * LANES:(j + 1) * LANES] * scale
        zk = z[:, d_attn + j * LANES:d_attn + (j + 1) * LANES]
        zv = z[:, 2 * d_attn + j * LANES:2 * d_attn + (j + 1) * LANES]
        for par in range(2):
            hh = 2 * j + par
            if par:
                zq, zk, zv = (pltpu.roll(a, HEAD_DIM, 1) for a in (zq, zk, zv))
            shift = (HEAD_DIM - AUG_STRIDE * hh) % LANES
            aq, ak = ((pltpu.roll(a, shift, 1) if shift else a) for a in (augq, augk))
            qa_ref[0, hh] = jnp.where(low, zq, jnp.where(extra, aq, 0.0)).astype(BF16)
            ka_ref[0, hh] = jnp.where(low, zk, jnp.where(extra, ak, 0.0)).astype(BF16)
            k_t = zk.T
            v_t = jnp.where(low, zv, vone).T
            va_ref[0, hh, 0] = v_t[:V_ROWS].astype(BF16)
            for pg in range(tm // LANES):
                kt_ref[0, pg, hh] = k_t[:HEAD_DIM, pg * LANES:(pg + 1) * LANES]
                vt_ref[0, pg, hh] = v_t[:HEAD_DIM, pg * LANES:(pg + 1) * LANES]


AUG_LANES = 6
AUG_STRIDE = 16


def _aug_constants(n_heads):
    assert n_heads * AUG_STRIDE <= LANES and AUG_LANES <= AUG_STRIDE
    place = np.zeros((3 * LANES, 2 * LANES), np.float32)
    const = np.zeros((1, 2 * LANES), np.float32)
    for h in range(n_heads):
        for piece in range(3):
            place[piece * LANES + h, h * AUG_STRIDE + piece] = 1.0
            const[0, h * AUG_STRIDE + 3 + piece] = 1.0
            const[0, LANES + h * AUG_STRIDE + piece] = 1.0
            place[piece * LANES + h, LANES + h * AUG_STRIDE + 3 + piece] = -1.0
    return jnp.asarray(place, BF16), jnp.asarray(const)


def _pre_prompt_call(x, mod_p, g_pre, w_qkv, w_f, b_f, w_u, n_heads, tm):
    b, s, d = x.shape
    d_attn = n_heads * HEAD_DIM
    d_ssm = w_u.shape[1]
    tri = jnp.asarray(np.tril(np.ones((tm, tm), np.float32)), BF16)
    w_f_hi = w_f.astype(BF16)
    w_f_lo = (w_f - w_f_hi.astype(F32)).astype(BF16)
    place, place_const = _aug_constants(n_heads)
    pages = tm // LANES
    const = lambda *shape: pl.BlockSpec(shape, lambda bi, ti: (0,) * len(shape))
    rows = lambda width: pl.BlockSpec((1, tm, width), lambda bi, ti: (bi, ti, 0))
    heads = pl.BlockSpec((1, n_heads, tm, LANES), lambda bi, ti: (bi, 0, ti, 0))
    heads_t = pl.BlockSpec((1, n_heads, 1, V_ROWS, tm), lambda bi, ti: (bi, 0, ti, 0, 0))
    paged = pl.BlockSpec((1, pages, n_heads, HEAD_DIM, LANES), lambda bi, ti: (bi, ti, 0, 0, 0))
    paged_shape = jax.ShapeDtypeStruct((b, s // LANES, n_heads, HEAD_DIM, LANES), F32)
    aug_shape = jax.ShapeDtypeStruct((b, n_heads, s, LANES), BF16)
    aug_t_shape = jax.ShapeDtypeStruct((b, n_heads, s // tm, V_ROWS, tm), BF16)
    return pl.pallas_call(
        functools.partial(_pre_prompt_kernel, n_heads=n_heads, d_attn=d_attn),
        grid=(b, s // tm),
        in_specs=[rows(d),
                  pl.BlockSpec((1, 6, 1, d), lambda bi, ti: (bi, 0, 0, 0)),
                  const(1, d), const(d, 3 * d_attn), const(d, LANES), const(d, LANES), const(1, LANES),
                  const(d, d_ssm), const(tm, tm), const(3 * LANES, 2 * LANES), const(1, 2 * LANES)],
        out_specs=[paged, paged, pl.BlockSpec((1, pages, n_heads, LANES), lambda bi, ti: (bi, ti, 0, 0)),
                   heads, heads, heads_t, rows(d_ssm)],
        out_shape=[paged_shape, paged_shape, jax.ShapeDtypeStruct((b, s // LANES, n_heads, LANES), F32),
                   aug_shape, aug_shape, aug_t_shape, jax.ShapeDtypeStruct((b, s, d_ssm), F32)],
        scratch_shapes=[pltpu.VMEM((1, LANES), F32)],
        compiler_params=_params("arbitrary", "arbitrary"),
        name="pre_prompt",
    )(x, mod_p, g_pre, w_qkv, w_f_hi, w_f_lo, b_f, w_u, tri, place, place_const)


def _pre_sample_kernel(x_ref, mod_ref, g_ref, wqkv_ref, wf_ref, bf_ref, wu_ref, z_ref, lf_ref, u_ref):
    x = x_ref[...]
    h = _rms(x) * g_ref[...] * (1.0 + mod_ref[0, 1]) + mod_ref[0, 0]
    hb = h.astype(BF16)
    z_ref[...] = jnp.dot(hb, wqkv_ref[...], preferred_element_type=F32)
    u_ref[...] = jnp.dot(hb, wu_ref[...], preferred_element_type=F32)
    fl = jnp.dot(h, wf_ref[...], precision=HI, preferred_element_type=F32) + bf_ref[...]
    lf_ref[...] = _log_sigmoid(fl)


def _pre_sample_call(x, mod_s, g_pre, w_qkv, w_f, b_f, w_u):
    rows = x.shape[0]
    return pl.pallas_call(
        _pre_sample_kernel,
        out_shape=[jax.ShapeDtypeStruct((rows, w_qkv.shape[1]), F32),
                   jax.ShapeDtypeStruct((rows, LANES), F32),
                   jax.ShapeDtypeStruct((rows, w_u.shape[1]), F32)],
        compiler_params=pltpu.CompilerParams(vmem_limit_bytes=VMEM_LIMIT),
        name="pre_sample",
    )(x, mod_s, g_pre, w_qkv, w_f, b_f, w_u)


FLASH_HEADS = 4


def _flash_kernel(q_ref, k_ref, vt_ref, o_ref, st_ref, bmax_ref, m_ref, acc_ref, *, tq):
    qi = pl.program_id(2)
    heads = q_ref.shape[1]

    def put_scores(kb, buf):
        off = pl.multiple_of(kb * tq, tq)
        for h in range(heads):
            st = lax.dot_general(k_ref[0, h, pl.ds(off, tq), :], q_ref[0, h], (((1,), (1,)), ((), ())),
                                 preferred_element_type=F32)
            st_ref[h, buf] = st
            bmax_ref[h, buf] = jnp.max(st, axis=0, keepdims=True)

    def update(kb, buf, diagonal):
        for h in range(heads):
            st = st_ref[h, buf]
            if diagonal:
                key = lax.broadcasted_iota(jnp.int32, st.shape, 0)
                qry = lax.broadcasted_iota(jnp.int32, st.shape, 1)
                st = jnp.where(key <= qry, st, NEG)
                block_max = jnp.max(st, axis=0, keepdims=True)
            else:
                block_max = bmax_ref[h, buf]
            m = m_ref[h]
            m_new = jnp.maximum(m, block_max)
            acc_ref[h] = acc_ref[h] * jnp.exp2(m - m_new) + jnp.dot(
                vt_ref[0, h, kb], jnp.exp2(st - m_new).astype(BF16), preferred_element_type=F32)
            m_ref[h] = m_new

    def finish():
        for h in range(heads):
            acc = acc_ref[h]
            o = acc[:HEAD_DIM] * (1.0 / acc[HEAD_DIM:HEAD_DIM + 1, :])
            o_ref[0, h] = jnp.concatenate([o, jnp.zeros((LANES - HEAD_DIM, tq), F32)], axis=0).T.astype(BF16)

    m_ref[...] = jnp.full_like(m_ref, NEG)
    acc_ref[...] = jnp.zeros_like(acc_ref)
    put_scores(0, 0)

    def pair(j, carry):
        put_scores(2 * j + 1, 1)
        update(2 * j, 0, False)
        put_scores(2 * j + 2, 0)
        update(2 * j + 1, 1, False)
        return carry

    lax.fori_loop(0, qi // 2, pair, 0)

    @pl.when(qi % 2 == 1)
    def _():
        put_scores(qi, 1)
        update(qi - 1, 0, False)
        update(qi, 1, True)
        finish()

    @pl.when(qi % 2 == 0)
    def _():
        update(qi, 0, True)
        finish()


def _flash_call(qa, ka, vat, tq):
    b, h, s, _ = qa.shape
    assert vat.shape[-1] == tq, "value chunks are laid out per key block"
    nh = math.gcd(FLASH_HEADS, h)
    qspec = pl.BlockSpec((1, nh, tq, LANES), lambda bi, hi, qi: (bi, hi, qi, 0))
    once = pl.Buffered(1)
    kspec = pl.BlockSpec((1, nh, s, LANES), lambda bi, hi, qi: (bi, hi, 0, 0), pipeline_mode=once)
    vspec = pl.BlockSpec((1, nh, s // tq, V_ROWS, tq), lambda bi, hi, qi: (bi, hi, 0, 0, 0), pipeline_mode=once)
    return pl.pallas_call(
        functools.partial(_flash_kernel, tq=tq),
        grid=(b, h // nh, s // tq),
        in_specs=[qspec, kspec, vspec],
        out_specs=qspec,
        out_shape=jax.ShapeDtypeStruct((b, h, s, LANES), BF16),
        scratch_shapes=[pltpu.VMEM((nh, 2, tq, tq), F32), pltpu.VMEM((nh, 2, 1, tq), F32),
                        pltpu.VMEM((nh, 1, tq), F32),
                        pltpu.VMEM((nh, V_ROWS, tq), F32)],
        compiler_params=_params("arbitrary", "arbitrary", "arbitrary"),
        name="flash",
    )(qa, ka, vat)


def _cmul(ar, ai, br, bi):
    return ar * br - ai * bi, ar * bi + ai * br


def _ssm_discretize(a_re, a_im, log_dt, b_re, b_im, c_re, c_im):
    ar, ai = a_re.astype(F32), a_im.astype(F32)
    dt = jnp.exp(log_dt.astype(F32))[:, None]
    lam_dt = (ar * dt, ai * dt)
    mag = jnp.exp(lam_dt[0])
    lam_bar = (mag * jnp.cos(lam_dt[1]), mag * jnp.sin(lam_dt[1]))
    den = ar * ar + ai * ai
    nr, ni = lam_bar[0] - 1.0, lam_bar[1]
    coef = ((nr * ar + ni * ai) / den, (ni * ar - nr * ai) / den)
    b_bar = _cmul(coef[0][..., None], coef[1][..., None], b_re.astype(F32), b_im.astype(F32))
    c = (c_re.astype(F32), c_im.astype(F32))
    return lam_dt, lam_bar, b_bar, c


def _ssm_chunk_operators(lam_dt, b_bar, c, chunk):
    g, p = lam_dt[0].shape
    nt = g // GROUPS_PER_TILE
    steps = jnp.arange(chunk + 1, dtype=F32)[:, None, None]
    mag = jnp.exp(steps * lam_dt[0][None])
    pw = (mag * jnp.cos(steps * lam_dt[1][None]), mag * jnp.sin(steps * lam_dt[1][None]))
    gpt = GROUPS_PER_TILE
    group_of = lambda cols, width: jnp.asarray(
        (np.arange(cols)[None, :] // width) % gpt == np.arange(gpt)[:, None], F32)
    pb = _cmul(pw[0][:chunk, :, :, None], pw[1][:chunk, :, :, None], b_bar[0][None], b_bar[1][None])
    kmat = (jnp.einsum("ghp,dgpk->dkgh", c[0], pb[0], precision=HI)
            - jnp.einsum("ghp,dgpk->dkgh", c[1], pb[1], precision=HI))
    kc = kmat.reshape(chunk, SSM_GROUP, nt, LANES).transpose(2, 0, 1, 3)
    lag_blocks = (kc[:, :, None] * group_of(LANES, SSM_GROUP)[None, None, :, None, :]
                  ).reshape(nt, chunk, LANES, LANES).astype(BF16)
    rev = chunk - 1 - np.arange(chunk)
    wb = _cmul(pw[0][rev][..., None], pw[1][rev][..., None], b_bar[0][None], b_bar[1][None])
    wsm = jnp.stack([part.reshape(chunk, nt, gpt, p, SSM_GROUP) for part in wb])
    wsm = wsm.transpose(2, 1, 5, 0, 3, 4).reshape(nt, chunk, SSM_GROUP, 2 * gpt * p)
    to_state = (wsm[:, :, None] * group_of(2 * gpt * p, p)[None, None, :, None, :]
                ).reshape(nt, chunk * LANES, 2 * gpt * p).astype(BF16)
    cp = _cmul(c[0][None], c[1][None], pw[0][1:chunk + 1][:, :, None, :], pw[1][1:chunk + 1][:, :, None, :])
    csm = jnp.stack([part.reshape(chunk, nt, gpt, SSM_GROUP, p) for part in (cp[0], -cp[1])])
    csm = csm.transpose(2, 0, 5, 1, 3, 4).reshape(nt, 2, p, chunk * LANES)
    from_state = (csm[:, :, None] * group_of(chunk * LANES, SSM_GROUP)[None, None, :, None, :]
                  ).reshape(nt, 2 * gpt * p, chunk * LANES).astype(BF16)
    dec_re = pw[0][chunk].reshape(nt, 1, gpt * p)
    dec_im = pw[1][chunk].reshape(nt, 1, gpt * p)
    return lag_blocks, to_state, from_state, dec_re, dec_im


def _ssm_kernel(u_ref, lag_ref, g_ref, c_ref, are_ref, aim_ref, d_ref, y_ref, hfin_ref, xin_ref, xs_ref, st_ref,
                toep_ref, *, chunk, n_chunks):
    r = pl.program_id(2)

    @pl.when((pl.program_id(1) == 0) & (r == 0))
    def _():
        for l_in in range(chunk):
            for l_out in range(chunk):
                blk = lag_ref[0, l_out - l_in] if l_out >= l_in else jnp.zeros((LANES, LANES), BF16)
                toep_ref[l_in * LANES:(l_in + 1) * LANES, l_out * LANES:(l_out + 1) * LANES] = blk

    @pl.when(r == 0)
    def _():
        st_ref[...] = jnp.zeros_like(st_ref)

    half = st_ref.shape[1] // 2
    pieces = [u_ref[0, pl.ds(l, n_chunks, stride=chunk), :] for l in range(chunk)]
    u2 = jnp.concatenate([pc.astype(BF16) for pc in pieces], axis=1)
    xin_ref[...] = jnp.dot(u2, g_ref[0], preferred_element_type=F32)
    ar = are_ref[0]
    ai = aim_ref[0]

    def body(i, carry):
        xr, xi = carry
        base = pl.multiple_of(i * 8, 8)
        blk = xin_ref[pl.ds(base, 8), :]
        rows_r, rows_i = [], []
        for rr in range(8):
            rows_r.append(xr)
            rows_i.append(xi)
            xr, xi = (ar * xr - ai * xi + blk[rr:rr + 1, :half],
                      ar * xi + ai * xr + blk[rr:rr + 1, half:])
        xs_ref[pl.ds(base, 8), :] = jnp.concatenate(
            [jnp.concatenate(rows_r, axis=0), jnp.concatenate(rows_i, axis=0)], axis=1)
        return xr, xi

    xr, xi = lax.fori_loop(0, n_chunks // 8, body, (st_ref[:, :half], st_ref[:, half:]), unroll=True)
    st_ref[...] = jnp.concatenate([xr, xi], axis=1)
    y2 = (jnp.dot(u2, toep_ref[...], preferred_element_type=F32)
          + jnp.dot(xs_ref[...].astype(BF16), c_ref[0], preferred_element_type=F32))
    dd = d_ref[0]
    for l in range(chunk):
        y_ref[0, pl.ds(l, n_chunks, stride=chunk), :] = y2[:, l * LANES:(l + 1) * LANES] + dd * pieces[l]

    @pl.when(r == pl.num_programs(2) - 1)
    def _():
        hfin_ref[0, 0] = st_ref[...]


def _ssm_call(u, lag_blocks, to_state, from_state, dec_re, dec_im, d_skip, rows_per_step):
    b, s, d_ssm = u.shape
    nt = d_ssm // LANES
    chunk = lag_blocks.shape[1]
    n_chunks = rows_per_step // chunk
    nstate = to_state.shape[2]
    tile = lambda *shape: pl.BlockSpec((1,) + shape, lambda j, bi, r: (j, 0, 0))
    useq = pl.BlockSpec((1, rows_per_step, LANES), lambda j, bi, r: (bi, r, j))
    return pl.pallas_call(
        functools.partial(_ssm_kernel, chunk=chunk, n_chunks=n_chunks),
        grid=(nt, b, s // rows_per_step),
        in_specs=[useq, pl.BlockSpec((1, chunk, LANES, LANES), lambda j, bi, r: (j, 0, 0, 0)),
                  tile(chunk * LANES, nstate),
                  tile(nstate, chunk * LANES), tile(1, nstate // 2), tile(1, nstate // 2), tile(1, LANES)],
        out_specs=[useq, pl.BlockSpec((1, 1, 1, nstate), lambda j, bi, r: (j, bi, 0, 0))],
        out_shape=[jax.ShapeDtypeStruct((b, s, d_ssm), F32), jax.ShapeDtypeStruct((nt, b, 1, nstate), F32)],
        scratch_shapes=[pltpu.VMEM((n_chunks, nstate), F32), pltpu.VMEM((n_chunks, nstate), F32),
                        pltpu.VMEM((1, nstate), F32), pltpu.VMEM((chunk * LANES, chunk * LANES), BF16)],
        compiler_params=_params("arbitrary", "arbitrary", "arbitrary"),
        name="ssm",
    )(u, lag_blocks, to_state, from_state, dec_re, dec_im, d_skip)


def _ssm_step_kernel(u_ref, hre_ref, him_ref, bre_ref, bim_ref, lre_ref, lim_ref, cre_ref, cim_ref, d_ref,
                     y_ref, xre_ref, xim_ref):
    u = u_ref[...]
    hr, hi = hre_ref[...], him_ref[...]
    lr, li = lre_ref[...], lim_ref[...]
    xr = lr * hr - li * hi + jnp.dot(u, bre_ref[...], precision=HI, preferred_element_type=F32)
    xi = lr * hi + li * hr + jnp.dot(u, bim_ref[...], precision=HI, preferred_element_type=F32)
    xre_ref[...] = xr
    xim_ref[...] = xi
    y_ref[...] = (jnp.dot(xr, cre_ref[...], precision=HI, preferred_element_type=F32)
                  - jnp.dot(xi, cim_ref[...], precision=HI, preferred_element_type=F32) + d_ref[...] * u)


def _ssm_step_call(u, h_re, h_im, lam_bar, b_bar, c, d_skip):
    rows, d_ssm = u.shape
    g, p = lam_bar[0].shape
    eye = jnp.eye(g, dtype=F32)
    bmat = [jnp.einsum("gpk,gf->gkfp", part, eye, precision=HI).reshape(d_ssm, g * p) for part in b_bar]
    cmat = [jnp.einsum("ghp,gf->gpfh", part, eye, precision=HI).reshape(g * p, d_ssm) for part in c]
    return pl.pallas_call(
        _ssm_step_kernel,
        out_shape=[jax.ShapeDtypeStruct((rows, d_ssm), F32), jax.ShapeDtypeStruct((rows, g * p), F32),
                   jax.ShapeDtypeStruct((rows, g * p), F32)],
        compiler_params=pltpu.CompilerParams(vmem_limit_bytes=VMEM_LIMIT),
        name="ssm_step",
    )(u, h_re.reshape(rows, g * p), h_im.reshape(rows, g * p), bmat[0], bmat[1],
      lam_bar[0].reshape(1, g * p), lam_bar[1].reshape(1, g * p), cmat[0], cmat[1], d_skip.reshape(1, d_ssm))


DEC_PAGES = 32


def _dec_kernel(pt_ref, q_ref, kn_ref, qt_ref, vnt_ref, lfn_ref, sli_ref, *rest, n_heads, pages):
    del pt_ref
    kt_refs, vt_refs, lf_refs = rest[:pages], rest[pages:2 * pages], rest[2 * pages:3 * pages]
    o_ref, qb_ref, m_ref, l_ref, acc_ref, carry_ref = rest[3 * pages:]
    step = pl.program_id(1)
    scale = HEAD_DIM ** -0.5

    @pl.when(step == 0)
    def _():
        qt = qt_ref[0] * scale
        vnt = vnt_ref[0]
        lane = lax.broadcasted_iota(jnp.int32, (HEAD_DIM, LANES), 1)
        for h in range(n_heads):
            qb_ref[h] = jnp.broadcast_to(qt[:, h:h + 1], (HEAD_DIM, LANES))
            acc_ref[h] = jnp.where(lane == 0, jnp.broadcast_to(vnt[:, h:h + 1], (HEAD_DIM, LANES)), 0.0)
        m_ref[...] = jnp.sum(q_ref[0] * kn_ref[0], axis=1, keepdims=True) * scale
        l_ref[...] = jnp.ones_like(l_ref)
        carry_ref[...] = lfn_ref[0]

    lf_all = jnp.concatenate([r[0, 0] for r in lf_refs], axis=0)
    sli = sli_ref[...]
    incl = sum(jnp.dot(piece, sli, preferred_element_type=F32) for piece in _split3(lf_all))
    run = carry_ref[...]
    scores = [None] * pages
    for i in reversed(range(pages)):
        rows = slice(i * n_heads, (i + 1) * n_heads)
        bias = incl[rows] - lf_all[rows] + run
        run = run + incl[rows][:, 0:1]
        qk = [jnp.sum(kt_refs[i][0, 0, h] * qb_ref[h], axis=0, keepdims=True) for h in range(n_heads)]
        scores[i] = jnp.concatenate(qk, axis=0) + bias
    carry_ref[...] = run

    m_old = m_ref[...]
    m_new = jnp.maximum(m_old, jnp.max(functools.reduce(jnp.maximum, scores), axis=1, keepdims=True))
    corr = jnp.exp(m_old - m_new)
    probs = [jnp.exp(s - m_new) for s in scores]
    l_ref[...] = l_ref[...] * corr + jnp.sum(functools.reduce(jnp.add, probs), axis=1, keepdims=True)
    m_ref[...] = m_new
    for h in range(n_heads):
        a = acc_ref[h] * corr[h:h + 1, :]
        for i in range(pages):
            a = a + vt_refs[i][0, 0, h] * probs[i][h:h + 1, :]
        acc_ref[h] = a

    @pl.when(step == pl.num_programs(1) - 1)
    def _():
        inv = 1.0 / l_ref[...]
        ones = jnp.ones((8, LANES), BF16)
        for h in range(n_heads):
            sums = sum(lax.dot_general(ones, piece, (((1,), (1,)), ((), ())), preferred_element_type=F32)
                       for piece in _split3(acc_ref[h]))
            o_ref[0, h:h + 1, :] = sums[0:1, :] * inv[h:h + 1, :]


def _dec_call(page_table, q, kn, qt, vnt, lfn, cache_k, cache_v, cache_logf, pages):
    db, n_pages = page_table.shape
    _, _, page, n_heads, _ = cache_k.shape
    assert page == LANES and n_pages % pages == 0
    kt = jnp.transpose(cache_k, (0, 1, 3, 4, 2))
    vt = jnp.transpose(cache_v, (0, 1, 3, 4, 2))
    lft = jnp.transpose(cache_logf, (0, 1, 3, 2))
    sli = jnp.asarray(np.tril(np.ones((page, page), np.float32)), BF16)
    pt_flat = page_table.reshape(-1)

    def phys(i):
        return lambda bi, st, pt: pt[bi * n_pages + n_pages - pages * (st + 1) + i]

    per_b = lambda *shape: pl.BlockSpec((1,) + shape, lambda bi, st, pt: (bi,) + (0,) * len(shape))
    kv_spec = lambda i: pl.BlockSpec((1, 1, n_heads, HEAD_DIM, page),
                                     lambda bi, st, pt, f=phys(i): (0, f(bi, st, pt), 0, 0, 0))
    lf_spec = lambda i: pl.BlockSpec((1, 1, n_heads, page), lambda bi, st, pt, f=phys(i): (0, f(bi, st, pt), 0, 0))
    grid_spec = pltpu.PrefetchScalarGridSpec(
        num_scalar_prefetch=1,
        grid=(db, n_pages // pages),
        in_specs=([per_b(n_heads, HEAD_DIM), per_b(n_heads, HEAD_DIM), per_b(HEAD_DIM, n_heads),
                   per_b(HEAD_DIM, n_heads), per_b(n_heads, 1), pl.BlockSpec((page, page), lambda bi, st, pt: (0, 0))]
                  + [kv_spec(i) for i in range(pages)] + [kv_spec(i) for i in range(pages)]
                  + [lf_spec(i) for i in range(pages)]),
        out_specs=per_b(n_heads, HEAD_DIM),
        scratch_shapes=[pltpu.VMEM((n_heads, HEAD_DIM, page), F32), pltpu.VMEM((n_heads, 1), F32),
                        pltpu.VMEM((n_heads, 1), F32), pltpu.VMEM((n_heads, HEAD_DIM, page), F32),
                        pltpu.VMEM((n_heads, 1), F32)])
    return pl.pallas_call(
        functools.partial(_dec_kernel, n_heads=n_heads, pages=pages),
        grid_spec=grid_spec,
        out_shape=jax.ShapeDtypeStruct((db, n_heads, HEAD_DIM), F32),
        compiler_params=_params("arbitrary", "arbitrary"),
        name="dec",
    )(pt_flat, q, kn, qt, vnt, lfn, sli, *([kt] * pages), *([vt] * pages), *([lft] * pages))


def _post_kernel(x_ref, o_ref, ys_ref, mod_ref, wglu_ref, bglu_ref, gatt_ref, gssm_ref, wout_ref, gpm_ref,
                 gpf_ref, wgu_ref, wdn_ref, gpo_ref, y_ref, *, n_heads, ff_chunk):
    x = x_ref[0]
    pairs = []
    for j in range(n_heads // 2):
        even = o_ref[0, 2 * j].astype(F32)
        odd = o_ref[0, 2 * j + 1].astype(F32)
        pairs.append(even + pltpu.roll(odd, HEAD_DIM, 1))
    attn = jnp.concatenate(pairs, axis=1)
    gl = _gelu_tanh(ys_ref[0])
    gl = gl * _sigmoid(jnp.dot(gl.astype(BF16), wglu_ref[...], preferred_element_type=F32) + bglu_ref[...])
    mix = jnp.concatenate([_rms(attn) * gatt_ref[...], _rms(gl) * gssm_ref[...]], axis=1)
    mo = jnp.dot(mix.astype(BF16), wout_ref[...], preferred_element_type=F32)
    x1 = x + mod_ref[0, 2] * (_rms(mo) * gpm_ref[...])
    h2 = (_rms(x1) * gpf_ref[...] * (1.0 + mod_ref[0, 4]) + mod_ref[0, 3]).astype(BF16)
    d_ff = wdn_ref.shape[0]
    acc = jnp.zeros_like(x)
    for c0 in range(0, d_ff, ff_chunk):
        gate = jnp.dot(h2, wgu_ref[:, c0:c0 + ff_chunk], preferred_element_type=F32)
        up = jnp.dot(h2, wgu_ref[:, d_ff + c0:d_ff + c0 + ff_chunk], preferred_element_type=F32)
        acc = acc + jnp.dot((_silu(gate) * up).astype(BF16), wdn_ref[c0:c0 + ff_chunk, :],
                            preferred_element_type=F32)
    y_ref[0] = x1 + mod_ref[0, 5] * (_rms(acc) * gpo_ref[...])


def _post_call(x, o, ys, mod, w_glu, b_glu, g_attn, g_ssm, w_out, g_post_mix, g_pre_ffn, w_gate_up, w_down,
               g_post_ffn, tm, ff_chunk):
    nb, s, d = x.shape
    n_heads = o.shape[1]
    d_ssm = ys.shape[2]
    mod_rows = mod.shape[2]
    const = lambda a: pl.BlockSpec(a.shape, lambda bi, ti: (0,) * a.ndim, pipeline_mode=pl.Buffered(1))
    rows = lambda width: pl.BlockSpec((1, tm, width), lambda bi, ti: (bi, ti, 0))
    mod_spec = (pl.BlockSpec((1, 6, 1, d), lambda bi, ti: (bi, 0, 0, 0)) if mod_rows == 1
                else pl.BlockSpec((1, 6, tm, d), lambda bi, ti: (bi, 0, ti, 0)))
    weights = (w_glu, b_glu, g_attn, g_ssm, w_out, g_post_mix, g_pre_ffn, w_gate_up, w_down, g_post_ffn)
    return pl.pallas_call(
        functools.partial(_post_kernel, n_heads=n_heads, ff_chunk=ff_chunk),
        grid=(nb, s // tm),
        in_specs=[rows(d), pl.BlockSpec((1, n_heads, tm, LANES), lambda bi, ti: (bi, 0, ti, 0)), rows(d_ssm),
                  mod_spec] + [const(w) for w in weights],
        out_specs=rows(d),
        out_shape=jax.ShapeDtypeStruct((nb, s, d), F32),
        compiler_params=_params("arbitrary", "arbitrary"),
        name="post",
    )(x, o, ys, mod, *weights)


def _row(v):
    return v.reshape(1, -1).astype(F32)


def _layer(xp, xs, cp, cs, cache_k, cache_v, cache_logf, h_re, h_im, page_table, w):
    b, s, d = xp.shape
    db = xs.shape[0]
    n_heads = w["b_f"].shape[0]
    d_attn = n_heads * HEAD_DIM
    d_ff = w["w_down"].shape[0]
    g, p = w["a_re"].shape
    d_ssm = g * SSM_GROUP

    w_in = w["w_in"]
    w_qkv = w_in[:, :3 * d_attn].astype(BF16)
    w_f = jnp.pad(w_in[:, 3 * d_attn:3 * d_attn + n_heads].astype(F32), ((0, 0), (0, LANES - n_heads)))
    b_f = jnp.pad(_row(w["b_f"]), ((0, 0), (0, LANES - n_heads)))
    w_u = w_in[:, 3 * d_attn + n_heads:].astype(BF16)
    post_w = (w["w_glu"].astype(BF16), _row(w["b_glu"]), _row(w["g_attn_out"]), _row(w["g_ssm_out"]),
              w["w_out"].astype(BF16), _row(w["g_post_mix"]), _row(w["g_pre_ffn"]), w["w_gate_up"].astype(BF16),
              w["w_down"].astype(BF16), _row(w["g_post_ffn"]))
    g_pre = _row(w["g_pre_mix"])

    n_cond = b + db
    pad_rows = -n_cond % 8
    c_all = jnp.pad(jnp.concatenate([cp, cs], axis=0).astype(F32), ((0, pad_rows), (0, 0)))
    mod = _mod_call(c_all, w["w_ada"].astype(F32), _row(w["b_ada"]))
    mod_p = mod[:b].reshape(b, 6, 1, d)
    mod_s = mod[b:n_cond].reshape(db, 6, d).transpose(1, 0, 2)[None]

    tm = min(ROW_TILE, s)
    assert s % tm == 0 and tm % LANES == 0 and cache_k.shape[2] == LANES
    k_p, v_p, lf_p, qa, ka, va, u_p = _pre_prompt_call(xp, mod_p, g_pre, w_qkv, w_f, b_f, w_u, n_heads, tm)
    o_p = _flash_call(qa, ka, va, tm)
    lam_dt, lam_bar, b_bar, c = _ssm_discretize(w["a_re"], w["a_im"], w["log_dt"], w["b_re"], w["b_im"],
                                                w["c_re"], w["c_im"])
    lag_blocks, to_state, from_state, dec_re, dec_im = _ssm_chunk_operators(lam_dt, b_bar, c, SSM_CHUNK)
    nt = d_ssm // LANES
    ys_p, hfin = _ssm_call(u_p, lag_blocks, to_state, from_state, dec_re, dec_im,
                           w["d_skip"].astype(F32).reshape(nt, 1, LANES), math.gcd(SSM_ROWS, s))
    y_p = _post_call(xp, o_p, ys_p, mod_p, *post_w, tm=tm, ff_chunk=math.gcd(FFN_CHUNK, d_ff))
    half = hfin.shape[-1] // 2
    hre_p = hfin[:, :, 0, :half].transpose(1, 0, 2).reshape(b, g, p)
    him_p = hfin[:, :, 0, half:].transpose(1, 0, 2).reshape(b, g, p)

    xs2 = xs.reshape(db, d)
    z_s, lf_s, u_s = _pre_sample_call(xs2, mod_s, g_pre, w_qkv, w_f, b_f, w_u)
    q_s = z_s[:, :d_attn].reshape(db, n_heads, HEAD_DIM)
    k_s = z_s[:, d_attn:2 * d_attn].reshape(db, n_heads, HEAD_DIM)
    v_s = z_s[:, 2 * d_attn:].reshape(db, n_heads, HEAD_DIM)
    lfn = lf_s[:, :n_heads]
    n_pages = page_table.shape[1]
    o_s = _dec_call(page_table, q_s, k_s, q_s.transpose(0, 2, 1), v_s.transpose(0, 2, 1),
                    lfn.reshape(db, n_heads, 1), cache_k, cache_v, cache_logf, math.gcd(DEC_PAGES, n_pages))
    o_s = jnp.pad(o_s.transpose(1, 0, 2), ((0, 0), (0, 0), (0, LANES - HEAD_DIM))).astype(BF16)[None]
    ys_s, hre_s, him_s = _ssm_step_call(u_s, h_re, h_im, lam_bar, b_bar, c, w["d_skip"].astype(F32))
    y_s = _post_call(xs2[None], o_s, ys_s[None], mod_s, *post_w, tm=db, ff_chunk=math.gcd(FFN_CHUNK, d_ff))

    return dict(
        y_p=y_p, y_s=y_s.reshape(db, 1, d),
        k_p=k_p.transpose(0, 1, 4, 2, 3), v_p=v_p.transpose(0, 1, 4, 2, 3),
        f_p=lf_p.transpose(0, 1, 3, 2), r_p=hre_p, i_p=him_p,
        k_s=k_s.reshape(db, 1, n_heads, HEAD_DIM), v_s=v_s.reshape(db, 1, n_heads, HEAD_DIM),
        f_s=lfn.reshape(db, 1, n_heads), r_s=hre_s.reshape(db, g, p), i_s=him_s.reshape(db, g, p))


def kernel(x_prompt, x_sample, c_prompt, c_sample, cache_k, cache_v, cache_logf, state_ssm_re, state_ssm_im,
           page_table, w_ada, b_ada, g_pre_mix, g_post_mix, g_pre_ffn, g_post_ffn, w_in, b_f, a_re, a_im,
           log_dt, b_re, b_im, c_re, c_im, d_skip, w_glu, b_glu, g_attn_out, g_ssm_out, w_out, w_gate_up, w_down):
    depth = w_in.shape[0]
    assert depth == 1 and x_sample.shape[1] == 1, "single layer, one new token per sequence"
    weights = dict(w_ada=w_ada, b_ada=b_ada, g_pre_mix=g_pre_mix, g_post_mix=g_post_mix, g_pre_ffn=g_pre_ffn,
                   g_post_ffn=g_post_ffn, w_in=w_in, b_f=b_f, a_re=a_re, a_im=a_im, log_dt=log_dt, b_re=b_re,
                   b_im=b_im, c_re=c_re, c_im=c_im, d_skip=d_skip, w_glu=w_glu, b_glu=b_glu,
                   g_attn_out=g_attn_out, g_ssm_out=g_ssm_out, w_out=w_out, w_gate_up=w_gate_up, w_down=w_down)
    w0 = {name: val[0] for name, val in weights.items()}
    r = _layer(x_prompt.astype(F32), x_sample.astype(F32), c_prompt, c_sample, cache_k, cache_v, cache_logf,
               state_ssm_re[0], state_ssm_im[0], page_table, w0)
    stack = lambda a: a[None]
    return (r["y_p"].astype(x_prompt.dtype), r["y_s"].astype(x_sample.dtype),
            stack(r["k_p"]), stack(r["v_p"]), stack(r["f_p"]), stack(r["r_p"]), stack(r["i_p"]),
            stack(r["k_s"]), stack(r["v_s"]), stack(r["f_s"]), stack(r["r_s"]), stack(r["i_s"]))
```
